```python
import math
import jax, jax.numpy as jnp
from jax import lax
import numpy as np

D_MODEL = 1024
BATCH = 16
SEQ = 256
DEPTH = 4
DEC_BATCH = 8
DEC_SEQ = 2048
PAST_LEN = 256

GRID_W = 64
HEAD_DIM = 64
BLOCK = 128
MIX_W = 512
N_BRANCH = 3
A_HEADS = 4
A_VDIM = 2 * HEAD_DIM
B_HEADS = 8
B_KV_HEADS = 2
B_GROUP = B_HEADS // B_KV_HEADS
WINDOW = 128
C_HEADS = 8
C_Q_LORA = 256
C_KV_LORA = 128
C_NOPE = 64
C_ROPE = 32
C_VDIM = 64
D_FF = 2816
N_EXPERTS = 8
TOP_K = 2
D_FF_EXPERT = 1408
N_DENSE = (DEPTH + 1) // 2
N_MOE = DEPTH // 2
ROPE_THETA = 10000.0
EPS = 1e-6
NEG_INF = -1e30
IN_SPLITS = (A_HEADS * 2 * HEAD_DIM, A_HEADS * 2 * HEAD_DIM, A_HEADS * A_VDIM,
             B_HEADS * HEAD_DIM, B_KV_HEADS * HEAD_DIM, B_KV_HEADS * HEAD_DIM,
             C_Q_LORA, C_KV_LORA, C_ROPE, N_BRANCH * D_MODEL)
IN_W = sum(IN_SPLITS)

kernel_name = "hybrid_diffusion_prefix_trunk_step"


def rmsnorm(x, g):
    xf = x.astype(jnp.float32)
    xf = xf * lax.rsqrt(jnp.mean(jnp.square(xf), axis=-1, keepdims=True) + EPS)
    return (xf * g.astype(jnp.float32)).astype(x.dtype)


def split_cols(p):
    out, off = [], 0
    for w in IN_SPLITS:
        out.append(p[..., off:off + w])
        off += w
    return out


def axial_rope_tables(n_tokens, rot_dim):
    rows_n = n_tokens // GRID_W
    row, col = jnp.meshgrid(jnp.arange(rows_n), jnp.arange(GRID_W), indexing="ij")
    row = row.reshape(-1).astype(jnp.float32)
    col = col.reshape(-1).astype(jnp.float32)
    n_freq = rot_dim // 4
    inv = jnp.power(ROPE_THETA, -jnp.arange(n_freq, dtype=jnp.float32) / n_freq)
    ang = jnp.concatenate([row[:, None] * inv, col[:, None] * inv], axis=-1)
    return jnp.cos(ang), jnp.sin(ang)


def apply_rope(x, cos, sin):
    half = x.shape[-1] // 2
    shp = (cos.shape[0],) + (1,) * (x.ndim - 3) + (half,)
    c = cos.reshape(shp).astype(x.dtype)
    s = sin.reshape(shp).astype(x.dtype)
    x1, x2 = x[..., :half], x[..., half:]
    return jnp.concatenate([x1 * c - x2 * s, x2 * c + x1 * s], axis=-1)


def sweep_query_blocks(fn, *qs):
    b, s = qs[0].shape[:2]
    nb = s // BLOCK
    blocks = tuple(jnp.swapaxes(q.reshape((b, nb, BLOCK) + q.shape[2:]), 0, 1) for q in qs)
    out = lax.map(lambda blk: fn(*blk), blocks)
    out = jnp.swapaxes(out, 0, 1)
    return out.reshape((b, s) + out.shape[3:])


def sink_softmax(s, sink):
    sink = sink.astype(jnp.float32)
    m = jnp.maximum(jnp.max(s, axis=-1, keepdims=True), sink)
    e = jnp.exp(s - m)
    return e / (jnp.sum(e, axis=-1, keepdims=True) + jnp.exp(sink - m))


def diff_attend(q, k, v, lam, lam_init, g):
    s = jnp.einsum("bqhmd,bkhmd->bhmqk", q, k, preferred_element_type=jnp.float32) * (HEAD_DIM ** -0.5)
    p = jax.nn.softmax(s, axis=-1)
    a = p[:, :, 0] - lam * p[:, :, 1]
    o = jnp.einsum("bhqk,bkhd->bqhd", a.astype(v.dtype), v)
    return rmsnorm(o, g) * (1.0 - lam_init)


def sink_attend(q, k, v, sink):
    s = jnp.einsum("bqhgd,bkhd->bhgqk", q, k, preferred_element_type=jnp.float32) * (HEAD_DIM ** -0.5)
    p = sink_softmax(s, sink[None, :, :, None, None])
    return jnp.einsum("bhgqk,bkhd->bqhgd", p.astype(v.dtype), v)


def window_attend_latent(q, k, v, k_ctx, v_ctx, sink):
    b, s_len = q.shape[:2]
    nb = s_len // BLOCK
    qb = q.reshape(b, nb, BLOCK, B_KV_HEADS, B_GROUP, HEAD_DIM)
    pad = ((0, 0), (BLOCK, BLOCK), (0, 0), (0, 0))
    kp = jnp.pad(k, pad).reshape(b, nb + 2, BLOCK, B_KV_HEADS, HEAD_DIM)
    vp = jnp.pad(v, pad).reshape(b, nb + 2, BLOCK, B_KV_HEADS, HEAD_DIM)
    kw = jnp.concatenate([kp[:, :nb], kp[:, 1:nb + 1], kp[:, 2:]], axis=2)
    vw = jnp.concatenate([vp[:, :nb], vp[:, 1:nb + 1], vp[:, 2:]], axis=2)
    rel_q = jnp.arange(BLOCK)[:, None] + BLOCK
    rel_k = jnp.arange(3 * BLOCK)[None, :]
    abs_k = jnp.arange(nb)[:, None, None] * BLOCK + rel_k[None] - BLOCK
    mask = (jnp.abs(rel_q - rel_k)[None] <= WINDOW) & (abs_k >= 0) & (abs_k < s_len)
    scale = HEAD_DIM ** -0.5
    s_loc = jnp.einsum("bnqhgd,bnkhd->bnhgqk", qb, kw, preferred_element_type=jnp.float32) * scale
    s_loc = jnp.where(mask[None, :, None, None], s_loc, NEG_INF)
    s_ctx = jnp.einsum("bnqhgd,bkhd->bnhgqk", qb, k_ctx, preferred_element_type=jnp.float32) * scale
    p = sink_softmax(jnp.concatenate([s_loc, s_ctx], axis=-1), sink[None, None, :, :, None, None])
    p = p.astype(v.dtype)
    w = 3 * BLOCK
    o = (jnp.einsum("bnhgqk,bnkhd->bnqhgd", p[..., :w], vw)
         + jnp.einsum("bnhgqk,bkhd->bnqhgd", p[..., w:], v_ctx))
    return o.reshape(b, s_len, B_KV_HEADS, B_GROUP, HEAD_DIM)


def mla_attend(q_nope, q_rope, k_nope, k_rope, v):
    s = (jnp.einsum("bqhd,bkhd->bhqk", q_nope, k_nope, preferred_element_type=jnp.float32)
         + jnp.einsum("bqhr,bkr->bhqk", q_rope, k_rope, preferred_element_type=jnp.float32))
    p = jax.nn.softmax(s * ((C_NOPE + C_ROPE) ** -0.5), axis=-1)
    return jnp.einsum("bhqk,bkhd->bqhd", p.astype(v.dtype), v)


def setup_inputs(seed: int = 0) -> dict:
    key = jax.random.key(seed)
    keys = jax.random.split(key, 40)
    counter = [0]

    def nrm(shape, scale):
        k = keys[counter[0]]
        counter[0] += 1
        return jax.random.normal(k, shape, jnp.float32) * scale

    def gain(shape):
        return 1.0 + nrm(shape, 0.02)

    dm = D_MODEL
    return {
        "x_prompt": nrm((BATCH, SEQ, dm), 1.0),
        "x_sample": nrm((DEC_BATCH, DEC_SEQ, dm), 1.0),
        "cache_diff_k": nrm((DEC_BATCH, DEPTH, PAST_LEN, A_HEADS, 2 * HEAD_DIM), 1.0),
        "cache_diff_v": nrm((DEC_BATCH, DEPTH, PAST_LEN, A_HEADS, A_VDIM), 1.0),
        "cache_win_k": nrm((DEC_BATCH, DEPTH, PAST_LEN, B_KV_HEADS, HEAD_DIM), 1.0),
        "cache_win_v": nrm((DEC_BATCH, DEPTH, PAST_LEN, B_KV_HEADS, HEAD_DIM), 1.0),
        "cache_mla_ckv": nrm((DEC_BATCH, DEPTH, PAST_LEN, C_KV_LORA), 1.0),
        "cache_mla_krope": nrm((DEC_BATCH, DEPTH, PAST_LEN, C_ROPE), 1.0),
        "c": nrm((DEC_BATCH, dm), 1.0),
        "c_ctx": nrm((dm,), 1.0),
        "w_ada": nrm((DEPTH, dm, 6 * dm), dm ** -0.5),
        "b_ada": nrm((DEPTH, 6 * dm), 0.02),
        "norm_attn_g": gain((DEPTH, dm)),
        "norm_ffn_g": gain((DEPTH, dm)),
        "w_in": nrm((DEPTH, dm, IN_W), dm ** -0.5),
        "diff_lambda_q1": nrm((DEPTH, HEAD_DIM), 0.1),
        "diff_lambda_k1": nrm((DEPTH, HEAD_DIM), 0.1),
        "diff_lambda_q2": nrm((DEPTH, HEAD_DIM), 0.1),
        "diff_lambda_k2": nrm((DEPTH, HEAD_DIM), 0.1),
        "diff_subln_g": gain((DEPTH, A_VDIM)),
        "win_sink": nrm((DEPTH, B_HEADS), 0.5),
        "mla_q_norm_g": gain((DEPTH, C_Q_LORA)),
        "mla_w_q_up": nrm((DEPTH, C_Q_LORA, C_HEADS * (C_NOPE + C_ROPE)), C_Q_LORA ** -0.5),
        "mla_kv_norm_g": gain((DEPTH, C_KV_LORA)),
        "mla_w_kv_up": nrm((DEPTH, C_KV_LORA, C_HEADS * (C_NOPE + C_VDIM)), C_KV_LORA ** -0.5),
        "w_branch": nrm((DEPTH, N_BRANCH, MIX_W, dm), MIX_W ** -0.5),
        "w_out": nrm((DEPTH, dm, dm), dm ** -0.5),
        "ffn_w_gate": nrm((N_DENSE, dm, D_FF), dm ** -0.5),
        "ffn_w_up": nrm((N_DENSE, dm, D_FF), dm ** -0.5),
        "ffn_w_down": nrm((N_DENSE, D_FF, dm), D_FF ** -0.5),
        "moe_w_router": nrm((N_MOE, dm, N_EXPERTS), dm ** -0.5),
        "moe_w_gate": nrm((N_MOE, N_EXPERTS, dm, D_FF_EXPERT), dm ** -0.5),
        "moe_w_up": nrm((N_MOE, N_EXPERTS, dm, D_FF_EXPERT), dm ** -0.5),
        "moe_w_down": nrm((N_MOE, N_EXPERTS, D_FF_EXPERT, dm), D_FF_EXPERT ** -0.5),
        "final_norm_g": gain((dm,)),
    }


def reference(x_prompt, x_sample, cache_diff_k, cache_diff_v, cache_win_k, cache_win_v,
              cache_mla_ckv, cache_mla_krope, c, c_ctx, w_ada, b_ada, norm_attn_g, norm_ffn_g,
              w_in, diff_lambda_q1, diff_lambda_k1, diff_lambda_q2, diff_lambda_k2, diff_subln_g,
              win_sink, mla_q_norm_g, mla_w_q_up, mla_kv_norm_g, mla_w_kv_up, w_branch, w_out,
              ffn_w_gate, ffn_w_up, ffn_w_down, moe_w_router, moe_w_gate, moe_w_up, moe_w_down,
              final_norm_g):
    f32 = jnp.float32

    def modulation(cond, l):
        mod = jax.nn.silu(cond) @ w_ada[l] + b_ada[l]
        return mod.reshape(cond.shape[0], 1, 6, D_MODEL)

    def project(h, l):
        b, s = h.shape[:2]
        aq, ak, av, bq, bk, bv, cq, ckv, ckr, gates = split_cols(h @ w_in[l])
        aq = aq.reshape(b, s, A_HEADS, 2, HEAD_DIM)
        ak = ak.reshape(b, s, A_HEADS, 2, HEAD_DIM)
        av = av.reshape(b, s, A_HEADS, A_VDIM)
        bq = bq.reshape(b, s, B_KV_HEADS, B_GROUP, HEAD_DIM)
        bk = bk.reshape(b, s, B_KV_HEADS, HEAD_DIM)
        bv = bv.reshape(b, s, B_KV_HEADS, HEAD_DIM)
        cq = (rmsnorm(cq, mla_q_norm_g[l]) @ mla_w_q_up[l]).reshape(b, s, C_HEADS, C_NOPE + C_ROPE)
        ckv = rmsnorm(ckv, mla_kv_norm_g[l])
        return aq, ak, av, bq, bk, bv, cq[..., :C_NOPE], cq[..., C_NOPE:], ckv, ckr, gates

    def mla_expand(ckv, l):
        b, s = ckv.shape[:2]
        kv = (ckv @ mla_w_kv_up[l]).reshape(b, s, C_HEADS, C_NOPE + C_VDIM)
        return kv[..., :C_NOPE], kv[..., C_NOPE:]

    def diff_lambda(l):
        lam_init = 0.8 - 0.6 * math.exp(-0.3 * l)
        lam = (jnp.exp(jnp.sum(diff_lambda_q1[l].astype(f32) * diff_lambda_k1[l].astype(f32)))
               - jnp.exp(jnp.sum(diff_lambda_q2[l].astype(f32) * diff_lambda_k2[l].astype(f32)))
               + lam_init)
        return lam, lam_init

    def merge(gates, oa, ob, oc, l):
        b, s = oa.shape[:2]
        g = jax.nn.sigmoid(gates.astype(f32)).astype(oa.dtype).reshape(b, s, N_BRANCH, D_MODEL)
        o = jnp.stack([oa.reshape(b, s, MIX_W), ob.reshape(b, s, MIX_W), oc.reshape(b, s, MIX_W)], axis=2)
        y = jnp.einsum("bsnm,nmd->bsnd", o, w_branch[l])
        return jnp.sum(g * y, axis=2) @ w_out[l]

    def mix_context(h, l):
        aq, ak, av, bq, bk, bv, q_nope, q_rope, ckv, ckr, gates = project(h, l)
        lam, lam_init = diff_lambda(l)
        sub_g = diff_subln_g[l]
        oa = sweep_query_blocks(lambda q: diff_attend(q, ak, av, lam, lam_init, sub_g), aq)
        sink = win_sink[l].reshape(B_KV_HEADS, B_GROUP)
        ob = sweep_query_blocks(lambda q: sink_attend(q, bk, bv, sink), bq)
        k_nope, v_c = mla_expand(ckv, l)
        oc = sweep_query_blocks(lambda qn, qr: mla_attend(qn, qr, k_nope, ckr, v_c), q_nope, q_rope)
        b, s = h.shape[:2]
        ctx = (ak.reshape(b, s, A_HEADS, 2 * HEAD_DIM), av, bk, bv, ckv, ckr)
        return merge(gates, oa, ob, oc, l), ctx

    def mix_latent(h, l, ctx, rope_h, rope_r):
        ctx_ak, ctx_av, ctx_bk, ctx_bv, ctx_ckv, ctx_ckr = ctx
        b, s = h.shape[:2]
        n_ctx = ctx_ak.shape[1]
        cos_h, sin_h = rope_h
        cos_r, sin_r = rope_r
        aq, ak, av, bq, bk, bv, q_nope, q_rope, ckv, ckr, gates = project(h, l)
        lam, lam_init = diff_lambda(l)
        sub_g = diff_subln_g[l]
        aq = apply_rope(aq, cos_h, sin_h)
        ak_all = jnp.concatenate([apply_rope(ak, cos_h, sin_h),
                                  ctx_ak.reshape(b, n_ctx, A_HEADS, 2, HEAD_DIM)], axis=1)
        av_all = jnp.concatenate([av, ctx_av], axis=1)
        oa = sweep_query_blocks(lambda q: diff_attend(q, ak_all, av_all, lam, lam_init, sub_g), aq)
        sink = win_sink[l].reshape(B_KV_HEADS, B_GROUP)
        ob = window_attend_latent(apply_rope(bq, cos_h, sin_h), apply_rope(bk, cos_h, sin_h), bv,
                                  ctx_bk, ctx_bv, sink)
        q_rope = apply_rope(q_rope, cos_r, sin_r)
        k_nope, v_c = mla_expand(ckv, l)
        kn_ctx, v_ctx = mla_expand(ctx_ckv, l)
        k_nope_all = jnp.concatenate([k_nope, kn_ctx], axis=1)
        k_rope_all = jnp.concatenate([apply_rope(ckr, cos_r, sin_r), ctx_ckr], axis=1)
        v_all = jnp.concatenate([v_c, v_ctx], axis=1)
        oc = sweep_query_blocks(lambda qn, qr: mla_attend(qn, qr, k_nope_all, k_rope_all, v_all),
                                q_nope, q_rope)
        return merge(gates, oa, ob, oc, l)

    def channel_mix(h, l):
        i = l // 2
        if l % 2 == 0:
            return (jax.nn.silu(h @ ffn_w_gate[i]) * (h @ ffn_w_up[i])) @ ffn_w_down[i]
        logits = jnp.einsum("bsd,de->bse", h, moe_w_router[i], preferred_element_type=f32)
        top_v, top_i = lax.top_k(logits, TOP_K)
        top_w = jax.nn.softmax(top_v, axis=-1)
        gate = jnp.sum(jax.nn.one_hot(top_i, N_EXPERTS, dtype=f32) * top_w[..., None], axis=-2).astype(h.dtype)
        y = jnp.zeros_like(h)
        for e in range(N_EXPERTS):
            he = jax.nn.silu(h @ moe_w_gate[i, e]) * (h @ moe_w_up[i, e])
            y = y + gate[..., e:e + 1] * (he @ moe_w_down[i, e])
        return y

    xp = x_prompt
    ctx_layers = []
    for l in range(DEPTH):
        mod = modulation(c_ctx[None, :], l)
        h = rmsnorm(xp, norm_attn_g[l]) * (1.0 + mod[:, :, 1]) + mod[:, :, 0]
        y, ctx = mix_context(h, l)
        ctx_layers.append(ctx)
        xp = xp + mod[:, :, 2] * y
        h = rmsnorm(xp, norm_ffn_g[l]) * (1.0 + mod[:, :, 4]) + mod[:, :, 3]
        xp = xp + mod[:, :, 5] * channel_mix(h, l)
    y_prompt = rmsnorm(xp, final_norm_g)
    new_diff_k = jnp.stack([t[0] for t in ctx_layers], axis=1)
    new_diff_v = jnp.stack([t[1] for t in ctx_layers], axis=1)
    new_win_k = jnp.stack([t[2] for t in ctx_layers], axis=1)
    new_win_v = jnp.stack([t[3] for t in ctx_layers], axis=1)
    new_mla_ckv = jnp.stack([t[4] for t in ctx_layers], axis=1)
    new_mla_krope = jnp.stack([t[5] for t in ctx_layers], axis=1)

    xs = x_sample
    n_lat = xs.shape[1]
    rope_h = axial_rope_tables(n_lat, HEAD_DIM)
    rope_r = axial_rope_tables(n_lat, C_ROPE)
    for l in range(DEPTH):
        mod = modulation(c, l)
        ctx = (cache_diff_k[:, l], cache_diff_v[:, l], cache_win_k[:, l], cache_win_v[:, l],
               cache_mla_ckv[:, l], cache_mla_krope[:, l])
        h = rmsnorm(xs, norm_attn_g[l]) * (1.0 + mod[:, :, 1]) + mod[:, :, 0]
        xs = xs + mod[:, :, 2] * mix_latent(h, l, ctx, rope_h, rope_r)
        h = rmsnorm(xs, norm_ffn_g[l]) * (1.0 + mod[:, :, 4]) + mod[:, :, 3]
        xs = xs + mod[:, :, 5] * channel_mix(h, l)
    y_sample = rmsnorm(xs, final_norm_g)

    return (y_prompt, y_sample, new_diff_k, new_diff_v, new_win_k, new_win_v, new_mla_ckv, new_mla_krope)
```

```python
import functools
import math

import jax
import jax.numpy as jnp
from jax import lax
from jax.experimental import pallas as pl
from jax.experimental.pallas import tpu as pltpu

F32 = jnp.float32
BF16 = jnp.bfloat16

D = 1024
HD = 64
LANES = 128
GRID_W = 64
BLOCK = 128
WINDOW = 128
A_HEADS = 4
B_HEADS = 8
B_GROUP = 4
C_HEADS = 8
C_Q_LORA = 256
C_KV_LORA = 128
C_NOPE = 64
C_ROPE = 32
C_VDIM = 64
N_EXPERTS = 8
ROPE_THETA = 10000.0
EPS = 1e-6
NEG_INF = -1e30
MOD_ROWS = 16

TM = 512
TQ = 256
TQB = 512
VMEM_LIMIT = 56 * 1024 * 1024

CH_AQ, CH_AK, CH_AV, CH_BQ, CH_BK, CH_BV, CH_CQ, CH_CKV, CH_CKR = 0, 4, 8, 12, 20, 21, 22, 24, 25
N_CH = 26
NPA = N_CH * LANES
KR_OFF = C_NOPE


def _cparams(sem):
    return pltpu.CompilerParams(dimension_semantics=sem, vmem_limit_bytes=VMEM_LIMIT)


def _rms(x, g):
    return x * lax.rsqrt(jnp.mean(x * x, axis=-1, keepdims=True) + EPS) * g


def _sigmoid(x):
    return 1.0 / (1.0 + jnp.exp(-x))


def _dot(a, b):
    return jnp.dot(a, b, preferred_element_type=F32)


def _dot_t(a, b):
    return lax.dot_general(a, b, (((1,), (1,)), ((), ())), preferred_element_type=F32)


def _ada_kernel(c_ref, w_ref, b_ref, o_ref):
    c = c_ref[...]
    s = (c * _sigmoid(c)).astype(BF16)
    o_ref[...] = _dot(s, w_ref[...].astype(BF16)) + b_ref[...]


def _ada_mod(cond, w_ada, b_ada):
    depth, _, n = w_ada.shape
    tn = 1536
    return pl.pallas_call(
        _ada_kernel,
        out_shape=jax.ShapeDtypeStruct((depth, MOD_ROWS, n), F32),
        grid=(depth, n // tn),
        in_specs=[
            pl.BlockSpec((MOD_ROWS, D), lambda l, j: (0, 0)),
            pl.BlockSpec((None, D, tn), lambda l, j: (l, 0, j)),
            pl.BlockSpec((None, 1, tn), lambda l, j: (l, 0, j)),
        ],
        out_specs=pl.BlockSpec((None, MOD_ROWS, tn), lambda l, j: (l, 0, j)),
        compiler_params=_cparams(("parallel", "parallel")),
        name="ada_mod",
    )(cond, w_ada, b_ada.reshape(depth, 1, n))


def _mod_spec(l, k, row_of_tile):
    return pl.BlockSpec((None, None, 1, D), lambda i: (l, row_of_tile(i), 0, k))


def _norm_mod(x, g_ref, shift_ref, scale_ref):
    return _rms(x, g_ref[...]) * (1.0 + scale_ref[...]) + shift_ref[...]


def _proj_kernel(x_ref, shift_ref, scale_ref, g_ref, w_ref, o_ref):
    h = _norm_mod(x_ref[...], g_ref, shift_ref, scale_ref).astype(BF16)
    o_ref[...] = _dot(h, w_ref[...])


def _proj(x, mod4, norm_g3, w_a, l, row_of_tile):
    t = x.shape[0]
    return pl.pallas_call(
        _proj_kernel,
        out_shape=jax.ShapeDtypeStruct((t, NPA), F32),
        grid=(t // TM,),
        in_specs=[
            pl.BlockSpec((TM, D), lambda i: (i, 0)),
            _mod_spec(l, 0, row_of_tile),
            _mod_spec(l, 1, row_of_tile),
            pl.BlockSpec((None, 1, D), lambda i: (l, 0, 0)),
            pl.BlockSpec((None, D, NPA), lambda i: (l, 0, 0)),
        ],
        out_specs=pl.BlockSpec((TM, NPA), lambda i: (i, 0)),
        compiler_params=_cparams(("parallel",)),
        name="proj",
    )(x, mod4, mod4, norm_g3, w_a)


def _rope(x, c, sa, sb, half):
    return x * c + pltpu.roll(x, LANES - half, 1) * sa + pltpu.roll(x, half, 1) * sb


def _prep_kernel(pa_ref, hc_ref, hsa_ref, hsb_ref, rc_ref, rsa_ref, rsb_ref, gq_ref, gkv_ref,
                 wq_ref, wkn_ref, wv_ref,
                 qa_ref, ka_ref, va_ref, qb_ref, kb_ref, vb_ref, qc_ref, kc_ref, vc_ref, ckvn_ref):
    hc, hsa, hsb = hc_ref[...], hsa_ref[...], hsb_ref[...]
    rc, rsa, rsb = rc_ref[...], rsa_ref[...], rsb_ref[...]

    def chunk(c):
        return pa_ref[:, c * LANES:(c + 1) * LANES]

    def put(ref, c, val):
        ref[:, c * LANES:(c + 1) * LANES] = val.astype(ref.dtype)

    q_scale = HD ** -0.5
    for c in range(A_HEADS):
        put(qa_ref, c, _rope(chunk(CH_AQ + c), hc, hsa, hsb, HD // 2) * q_scale)
        put(ka_ref, c, _rope(chunk(CH_AK + c), hc, hsa, hsb, HD // 2))
        put(va_ref, c, chunk(CH_AV + c))
    for c in range(B_HEADS):
        put(qb_ref, c, _rope(chunk(CH_BQ + c), hc, hsa, hsb, HD // 2) * q_scale)
    put(kb_ref, 0, _rope(chunk(CH_BK), hc, hsa, hsb, HD // 2))
    put(vb_ref, 0, chunk(CH_BV))

    cq = pa_ref[:, CH_CQ * LANES:(CH_CQ + 2) * LANES]
    cqn = _rms(cq, gq_ref[...]).astype(BF16)
    qup = _dot(cqn, wq_ref[...])
    c_scale = (C_NOPE + C_ROPE) ** -0.5
    for c in range(C_HEADS):
        put(qc_ref, c, _rope(qup[:, c * LANES:(c + 1) * LANES], rc, rsa, rsb, C_ROPE // 2) * c_scale)

    ckvn = _rms(chunk(CH_CKV), gkv_ref[...])
    ckvn_ref[...] = ckvn
    ckvb = ckvn.astype(BF16)
    kn = _dot(ckvb, wkn_ref[...])
    kr = _rope(chunk(CH_CKR), rc, rsa, rsb, C_ROPE // 2)
    for c in range(C_HEADS):
        put(kc_ref, c, kn[:, c * LANES:(c + 1) * LANES] + kr)
    vc_ref[...] = _dot(ckvb, wv_ref[...]).astype(BF16)


def _prep(pa, tabs, gq3, gkv3, wq, wkn, wv, l, tab_block):
    t = pa.shape[0]
    tab_spec = pl.BlockSpec((TM, LANES), lambda i: (tab_block(i), 0))
    wide = lambda n: pl.BlockSpec((TM, n), lambda i: (i, 0))
    outs = [(4 * LANES, BF16)] * 3 + [(8 * LANES, BF16), (LANES, BF16), (LANES, BF16)] \
        + [(8 * LANES, BF16)] * 3 + [(LANES, F32)]
    return pl.pallas_call(
        _prep_kernel,
        out_shape=[jax.ShapeDtypeStruct((t, n), dt) for n, dt in outs],
        grid=(t // TM,),
        in_specs=[wide(NPA)] + [tab_spec] * 6 + [
            pl.BlockSpec((None, 1, C_Q_LORA), lambda i: (l, 0, 0)),
            pl.BlockSpec((None, 1, C_KV_LORA), lambda i: (l, 0, 0)),
            pl.BlockSpec((None, C_Q_LORA, C_HEADS * LANES), lambda i: (l, 0, 0)),
            pl.BlockSpec((None, C_KV_LORA, C_HEADS * LANES), lambda i: (l, 0, 0)),
            pl.BlockSpec((None, C_KV_LORA, C_HEADS * LANES), lambda i: (l, 0, 0)),
        ],
        out_specs=[wide(n) for n, _ in outs],
        compiler_params=_cparams(("parallel",)),
        name="prep",
    )(pa, *tabs, gq3, gkv3, wq, wkn, wv)


def _softmax_parts(scores):
    m = functools.reduce(jnp.maximum, [jnp.max(s, axis=-1, keepdims=True) for s in scores])
    es = [jnp.exp(s - m) for s in scores]
    den = functools.reduce(lambda a, b: a + b, [jnp.sum(e, axis=-1, keepdims=True) for e in es])
    return es, 1.0 / den


def _attn_a_kernel(lam_init, has_ctx, q_ref, k_ref, v_ref, *rest):
    if has_ctx:
        kc_ref, vc_ref, lq1_ref, lk1_ref, lq2_ref, lk2_ref, sg_ref, o_ref = rest
    else:
        lq1_ref, lk1_ref, lq2_ref, lk2_ref, sg_ref, o_ref = rest
    q = q_ref[...]
    lane = lax.broadcasted_iota(jnp.int32, q.shape, 1)
    zero = jnp.zeros_like(q)
    q1 = jnp.where(lane < HD, q, zero)
    q2 = jnp.where(lane >= HD, q, zero)
    keys = [k_ref[...]]
    vals = [v_ref[...]]
    if has_ctx:
        keys.append(kc_ref[...].astype(BF16))
        vals.append(vc_ref[...].astype(BF16))
    e1, r1 = _softmax_parts([_dot_t(q1, k) for k in keys])
    e2, r2 = _softmax_parts([_dot_t(q2, k) for k in keys])
    lam = (jnp.exp(jnp.sum(lq1_ref[...] * lk1_ref[...], axis=-1, keepdims=True))
           - jnp.exp(jnp.sum(lq2_ref[...] * lk2_ref[...], axis=-1, keepdims=True)) + lam_init)
    r2 = r2 * lam
    o = None
    for a1, a2, v in zip(e1, e2, vals):
        part = _dot((a1 * r1 - a2 * r2).astype(BF16), v)
        o = part if o is None else o + part
    o_ref[...] = (_rms(o, sg_ref[...]) * (1.0 - lam_init)).astype(o_ref.dtype)


def _attn_a(qa, ka, va, ctx, lam4, subg3, l, nb, s):
    t = qa.shape[0]
    tq = min(TQ, s)
    nq = s // tq
    lam_init = 0.8 - 0.6 * math.exp(-0.3 * l)
    in_specs = [
        pl.BlockSpec((tq, LANES), lambda b, h, i: (b * nq + i, h)),
        pl.BlockSpec((s, LANES), lambda b, h, i: (b, h)),
        pl.BlockSpec((s, LANES), lambda b, h, i: (b, h)),
    ]
    args = [qa, ka, va]
    if ctx is not None:
        past = ctx[0].shape[2]
        in_specs += [pl.BlockSpec((None, None, past, LANES), lambda b, h, i: (b, l, 0, h))] * 2
        args += list(ctx)
    in_specs += [pl.BlockSpec((None, 1, HD), lambda b, h, i: (l, 0, 0))] * 4
    in_specs += [pl.BlockSpec((None, 1, LANES), lambda b, h, i: (l, 0, 0))]
    return pl.pallas_call(
        functools.partial(_attn_a_kernel, lam_init, ctx is not None),
        out_shape=jax.ShapeDtypeStruct((t, A_HEADS * LANES), BF16),
        grid=(nb, A_HEADS, nq),
        in_specs=in_specs,
        out_specs=pl.BlockSpec((tq, LANES), lambda b, h, i: (b * nq + i, h)),
        compiler_params=_cparams(("parallel", "parallel", "arbitrary")),
        name="attn_a_lat" if ctx is not None else "attn_a_ctx",
    )(*args, *lam4, subg3)


def _sink_attend(qs, keys, vals, masks, sinkv):
    scores = []
    for k, mask in zip(keys, masks):
        s = _dot_t(qs, k)
        scores.append(s if mask is None else jnp.where(mask, s, NEG_INF))
    m = functools.reduce(jnp.maximum, [jnp.max(s, axis=-1, keepdims=True) for s in scores] + [sinkv])
    es = [jnp.exp(s - m) for s in scores]
    den = functools.reduce(lambda a, b: a + b,
                           [jnp.sum(e, axis=-1, keepdims=True) for e in es] + [jnp.exp(sinkv - m)])
    r = 1.0 / den
    o = None
    for e, v in zip(es, vals):
        part = _dot((e * r).astype(BF16), v)
        o = part if o is None else o + part
    return o


def _stack_group(q_ref, row0, rows, g):
    return jnp.concatenate(
        [q_ref[pl.ds(row0, rows), (B_GROUP * g + j) * LANES:(B_GROUP * g + j + 1) * LANES]
         for j in range(B_GROUP)], axis=0)


def _sink_column(sink_ref, l, g, rows):
    row = lax.broadcasted_iota(jnp.int32, (B_GROUP * rows, 1), 0)
    col = jnp.full((B_GROUP * rows, 1), sink_ref[l, B_GROUP * g + B_GROUP - 1], F32)
    for j in range(B_GROUP - 2, -1, -1):
        col = jnp.where(row < (j + 1) * rows, sink_ref[l, B_GROUP * g + j], col)
    return col


def _store_group(o_ref, row0, rows, g, o):
    lane = lax.broadcasted_iota(jnp.int32, o.shape, 1)
    o = jnp.where((lane >= g * HD) & (lane < (g + 1) * HD), o, 0.0).astype(o_ref.dtype)
    for j in range(B_GROUP):
        c = B_GROUP * g + j
        o_ref[pl.ds(row0, rows), c * LANES:(c + 1) * LANES] = o[j * rows:(j + 1) * rows, :]


def _attn_b_ctx_kernel(l, sink_ref, q_ref, k_ref, v_ref, o_ref):
    rows = q_ref.shape[0]
    k, v = k_ref[...], v_ref[...]
    for g in range(B_HEADS // B_GROUP):
        qs = _stack_group(q_ref, 0, rows, g)
        o = _sink_attend(qs, [k], [v], [None], _sink_column(sink_ref, l, g, rows))
        _store_group(o_ref, 0, rows, g, o)


def _attn_b_lat_kernel(l, s_len, sink_ref, q_ref, k_ref, v_ref, kc_ref, vc_ref, o_ref):
    qi = pl.program_id(1)
    kc = kc_ref[...].astype(BF16)
    vc = vc_ref[...].astype(BF16)
    win = 3 * BLOCK
    for n in range(TQB // BLOCK):
        blk = qi * (TQB // BLOCK) + n
        ws = pl.multiple_of(jnp.clip((blk - 1) * BLOCK, 0, s_len - win), BLOCK)
        kw = k_ref[pl.ds(ws, win), :]
        vw = v_ref[pl.ds(ws, win), :]
        qpos = blk * BLOCK + lax.broadcasted_iota(jnp.int32, (B_GROUP * BLOCK, win), 0) % BLOCK
        kpos = ws + lax.broadcasted_iota(jnp.int32, (B_GROUP * BLOCK, win), 1)
        mask = jnp.abs(qpos - kpos) <= WINDOW
        for g in range(B_HEADS // B_GROUP):
            qs = _stack_group(q_ref, n * BLOCK, BLOCK, g)
            o = _sink_attend(qs, [kw, kc], [vw, vc], [mask, None], _sink_column(sink_ref, l, g, BLOCK))
            _store_group(o_ref, n * BLOCK, BLOCK, g, o)


def _attn_b(qb, kb, vb, ctx, win_sink, l, nb, s):
    t = qb.shape[0]
    smem = pl.BlockSpec(memory_space=pltpu.SMEM)
    if ctx is None:
        return pl.pallas_call(
            functools.partial(_attn_b_ctx_kernel, l),
            out_shape=jax.ShapeDtypeStruct((t, B_HEADS * LANES), BF16),
            grid=(nb,),
            in_specs=[smem,
                      pl.BlockSpec((s, B_HEADS * LANES), lambda b: (b, 0)),
                      pl.BlockSpec((s, LANES), lambda b: (b, 0)),
                      pl.BlockSpec((s, LANES), lambda b: (b, 0))],
            out_specs=pl.BlockSpec((s, B_HEADS * LANES), lambda b: (b, 0)),
            compiler_params=_cparams(("parallel",)),
            name="attn_b_ctx",
        )(win_sink, qb, kb, vb)
    assert s % TQB == 0 and s >= 3 * BLOCK
    nq = s // TQB
    past = ctx[0].shape[2]
    return pl.pallas_call(
        functools.partial(_attn_b_lat_kernel, l, s),
        out_shape=jax.ShapeDtypeStruct((t, B_HEADS * LANES), BF16),
        grid=(nb, nq),
        in_specs=[smem,
                  pl.BlockSpec((TQB, B_HEADS * LANES), lambda b, i: (b * nq + i, 0)),
                  pl.BlockSpec((s, LANES), lambda b, i: (b, 0)),
                  pl.BlockSpec((s, LANES), lambda b, i: (b, 0)),
                  pl.BlockSpec((None, None, past, LANES), lambda b, i: (b, l, 0, 0)),
                  pl.BlockSpec((None, None, past, LANES), lambda b, i: (b, l, 0, 0))],
        out_specs=pl.BlockSpec((TQB, B_HEADS * LANES), lambda b, i: (b * nq + i, 0)),
        compiler_params=_cparams(("parallel", "arbitrary")),
        name="attn_b_lat",
    )(win_sink, qb, kb, vb, *ctx)


def _attn_c_kernel(has_ctx, q_ref, k_ref, v_ref, *rest):
    if has_ctx:
        ckv_ref, krp_ref, wkn_ref, wv_ref, o_ref, kc_s, vc_s = rest

        @pl.when(pl.program_id(2) == 0)
        def _():
            ckv = ckv_ref[...].astype(BF16)
            kc_s[...] = (_dot(ckv, wkn_ref[...]) + krp_ref[...]).astype(BF16)
            vc_s[...] = _dot(ckv, wv_ref[...]).astype(BF16)

        keys, vals = [k_ref[...], kc_s[...]], [v_ref[...], vc_s[...]]
    else:
        (o_ref,) = rest
        keys, vals = [k_ref[...]], [v_ref[...]]
    q = q_ref[...]
    es, r = _softmax_parts([_dot_t(q, k) for k in keys])
    o = None
    for e, v in zip(es, vals):
        part = _dot((e * r).astype(BF16), v)
        o = part if o is None else o + part
    o_ref[...] = o.astype(o_ref.dtype)


def _attn_c(qc, kc, vc, ctx, wkn, wv, l, nb, s):
    t = qc.shape[0]
    tq = min(TQ, s)
    nq = s // tq
    in_specs = [
        pl.BlockSpec((tq, LANES), lambda b, h, i: (b * nq + i, h)),
        pl.BlockSpec((s, LANES), lambda b, h, i: (b, h)),
        pl.BlockSpec((s, LANES), lambda b, h, i: (b, h)),
    ]
    args = [qc, kc, vc]
    scratch = []
    if ctx is not None:
        past = ctx[0].shape[2]
        in_specs += [pl.BlockSpec((None, None, past, LANES), lambda b, h, i: (b, l, 0, 0))] * 2
        in_specs += [pl.BlockSpec((None, C_KV_LORA, LANES), lambda b, h, i: (l, 0, h))] * 2
        args += list(ctx) + [wkn, wv]
        scratch = [pltpu.VMEM((past, LANES), BF16)] * 2
    return pl.pallas_call(
        functools.partial(_attn_c_kernel, ctx is not None),
        out_shape=jax.ShapeDtypeStruct((t, C_HEADS * LANES), BF16),
        grid=(nb, C_HEADS, nq),
        in_specs=in_specs,
        out_specs=pl.BlockSpec((tq, LANES), lambda b, h, i: (b * nq + i, h)),
        scratch_shapes=scratch,
        compiler_params=_cparams(("parallel", "parallel", "arbitrary")),
        name="attn_c_lat" if ctx is not None else "attn_c_ctx",
    )(*args)


def _merge_kernel(x_ref, shift_ref, scale_ref, gate_ref, g_ref, wg_ref, oa_ref, ob_ref, oc_ref,
                  wba_ref, wbb_ref, wbc_ref, wo_ref, o_ref):
    x = x_ref[...]
    h = _norm_mod(x, g_ref, shift_ref, scale_ref).astype(BF16)
    mix = None
    for n, (br_ref, wb_ref) in enumerate(((oa_ref, wba_ref), (ob_ref, wbb_ref), (oc_ref, wbc_ref))):
        gate = _sigmoid(_dot(h, wg_ref[:, n * D:(n + 1) * D]))
        term = gate * _dot(br_ref[...], wb_ref[...])
        mix = term if mix is None else mix + term
    o_ref[...] = x + gate_ref[...] * _dot(mix.astype(BF16), wo_ref[...])


def _merge(x, mod4, norm_g3, wg, oa, ob, oc, wba, wbb, wbc, wo, l, row_of_tile):
    t = x.shape[0]
    row = lambda n: pl.BlockSpec((TM, n), lambda i: (i, 0))
    res = lambda r, c: pl.BlockSpec((None, r, c), lambda i: (l, 0, 0))
    return pl.pallas_call(
        _merge_kernel,
        out_shape=jax.ShapeDtypeStruct((t, D), F32),
        grid=(t // TM,),
        in_specs=[row(D), _mod_spec(l, 0, row_of_tile), _mod_spec(l, 1, row_of_tile),
                  _mod_spec(l, 2, row_of_tile), res(1, D), res(D, 3 * D),
                  row(A_HEADS * LANES), row(B_HEADS * LANES), row(C_HEADS * LANES),
                  res(A_HEADS * LANES, D), res(B_HEADS * LANES, D), res(C_HEADS * LANES, D), res(D, D)],
        out_specs=row(D),
        compiler_params=_cparams(("parallel",)),
        name="merge",
    )(x, mod4, mod4, mod4, norm_g3, wg, oa, ob, oc, wba, wbb, wbc, wo)


def _swiglu(h, wg, wu):
    a = _dot(h, wg)
    return (a * _sigmoid(a) * _dot(h, wu)).astype(BF16)


def _ffn_kernel(tf, x_ref, shift_ref, scale_ref, gate_ref, g_ref, wg_ref, wu_ref, wd_ref, o_ref, acc_ref):
    x = x_ref[...]
    h = _norm_mod(x, g_ref, shift_ref, scale_ref).astype(BF16)
    acc_ref[...] = jnp.zeros_like(acc_ref)

    def body(c, carry):
        off = pl.multiple_of(c * tf, tf)
        a = _swiglu(h, wg_ref[:, pl.ds(off, tf)], wu_ref[:, pl.ds(off, tf)])
        acc_ref[...] += _dot(a, wd_ref[pl.ds(off, tf), :])
        return carry

    lax.fori_loop(0, wg_ref.shape[1] // tf, body, 0)
    o_ref[...] = x + gate_ref[...] * acc_ref[...]


def _ffn(x, mod4, norm_g3, wg, wu, wd, l, idx, row_of_tile):
    t = x.shape[0]
    dff = wg.shape[2]
    tf = 256
    assert dff % tf == 0
    row = pl.BlockSpec((TM, D), lambda i: (i, 0))
    return pl.pallas_call(
        functools.partial(_ffn_kernel, tf),
        out_shape=jax.ShapeDtypeStruct((t, D), F32),
        grid=(t // TM,),
        in_specs=[row, _mod_spec(l, 3, row_of_tile), _mod_spec(l, 4, row_of_tile),
                  _mod_spec(l, 5, row_of_tile),
                  pl.BlockSpec((None, 1, D), lambda i: (l, 0, 0)),
                  pl.BlockSpec((None, D, dff), lambda i: (idx, 0, 0)),
                  pl.BlockSpec((None, D, dff), lambda i: (idx, 0, 0)),
                  pl.BlockSpec((None, dff, D), lambda i: (idx, 0, 0))],
        out_specs=row,
        scratch_shapes=[pltpu.VMEM((TM, D), F32)],
        compiler_params=_cparams(("parallel",)),
        name="ffn",
    )(x, mod4, mod4, mod4, norm_g3, wg, wu, wd)


def _top2_gate(logits):
    lane = lax.broadcasted_iota(jnp.int32, logits.shape, 1)
    valid = lane < N_EXPERTS
    lg = jnp.where(valid, logits, NEG_INF)
    m1 = jnp.max(lg, axis=-1, keepdims=True)
    i1 = jnp.min(jnp.where(lg == m1, lane, LANES), axis=-1, keepdims=True)
    rest = jnp.where(lane == i1, NEG_INF, lg)
    m2 = jnp.max(rest, axis=-1, keepdims=True)
    i2 = jnp.min(jnp.where(rest == m2, lane, LANES), axis=-1, keepdims=True)
    e2 = jnp.exp(m2 - m1)
    den = 1.0 + e2
    return jnp.where(lane == i1, 1.0 / den, jnp.where(lane == i2, e2 / den, 0.0))


def _moe_kernel(x_ref, shift_ref, scale_ref, gate_ref, g_ref, wr_ref, wg_ref, wu_ref, wd_ref, o_ref,
                h_s, gate_s, acc_s):
    e = pl.program_id(1)

    @pl.when(e == 0)
    def _():
        h = _norm_mod(x_ref[...], g_ref, shift_ref, scale_ref)
        h_s[...] = h.astype(BF16)
        logits = jnp.dot(h, wr_ref[...], preferred_element_type=F32, precision=lax.Precision.HIGHEST)
        gate_s[...] = _top2_gate(logits)
        acc_s[...] = jnp.zeros_like(acc_s)

    h = h_s[...]
    y = _dot(_swiglu(h, wg_ref[...], wu_ref[...]), wd_ref[...])
    lane = lax.broadcasted_iota(jnp.int32, gate_s.shape, 1)
    ge = jnp.sum(jnp.where(lane == e, gate_s[...], 0.0), axis=-1, keepdims=True)
    acc_s[...] += ge * y

    @pl.when(e == N_EXPERTS - 1)
    def _():
        o_ref[...] = x_ref[...] + gate_ref[...] * acc_s[...]


def _moe(x, mod4, norm_g3, wr, wg, wu, wd, l, idx, row_of_tile):
    t = x.shape[0]
    dfe = wg.shape[3]
    row = pl.BlockSpec((TM, D), lambda i, e: (i, 0))
    mod = lambda k: pl.BlockSpec((None, None, 1, D), lambda i, e: (l, row_of_tile(i), 0, k))
    return pl.pallas_call(
        _moe_kernel,
        out_shape=jax.ShapeDtypeStruct((t, D), F32),
        grid=(t // TM, N_EXPERTS),
        in_specs=[row, mod(3), mod(4), mod(5),
                  pl.BlockSpec((None, 1, D), lambda i, e: (l, 0, 0)),
                  pl.BlockSpec((None, D, LANES), lambda i, e: (idx, 0, 0)),
                  pl.BlockSpec((None, None, D, dfe), lambda i, e: (idx, e, 0, 0)),
                  pl.BlockSpec((None, None, D, dfe), lambda i, e: (idx, e, 0, 0)),
                  pl.BlockSpec((None, None, dfe, D), lambda i, e: (idx, e, 0, 0))],
        out_specs=row,
        scratch_shapes=[pltpu.VMEM((TM, D), BF16), pltpu.VMEM((TM, LANES), F32), pltpu.VMEM((TM, D), F32)],
        compiler_params=_cparams(("parallel", "arbitrary")),
        name="moe",
    )(x, mod4, mod4, mod4, norm_g3, wr, wg, wu, wd)


def _final_kernel(x_ref, g_ref, o_ref):
    o_ref[...] = _rms(x_ref[...], g_ref[...])


def _final_norm(x, g2):
    t = x.shape[0]
    return pl.pallas_call(
        _final_kernel,
        out_shape=jax.ShapeDtypeStruct((t, D), F32),
        grid=(t // TM,),
        in_specs=[pl.BlockSpec((TM, D), lambda i: (i, 0)), pl.BlockSpec((1, D), lambda i: (0, 0))],
        out_specs=pl.BlockSpec((TM, D), lambda i: (i, 0)),
        compiler_params=_cparams(("parallel",)),
        name="final_norm",
    )(x, g2)


def _relayout_w_in(w_in):
    depth = w_in.shape[0]
    z = lambda n: jnp.zeros((depth, D, n), w_in.dtype)
    aq, ak, av = w_in[..., 0:512], w_in[..., 512:1024], w_in[..., 1024:1536]
    bq, bk, bv = w_in[..., 1536:2048], w_in[..., 2048:2176], w_in[..., 2176:2304]
    cq, ckv, ckr = w_in[..., 2304:2560], w_in[..., 2560:2688], w_in[..., 2688:2720]
    gates = w_in[..., 2720:2720 + 3 * D]
    pieces = [aq, ak, av]
    for h in range(B_HEADS):
        qh = bq[..., h * HD:(h + 1) * HD]
        pieces += [qh, z(HD)] if h // B_GROUP == 0 else [z(HD), qh]
    pieces += [bk, bv, cq, ckv, z(KR_OFF), ckr, z(LANES - KR_OFF - C_ROPE)]
    w_a = jnp.concatenate(pieces, axis=-1).astype(BF16)
    assert w_a.shape[-1] == NPA
    return w_a, gates.astype(BF16)


def _relayout_mla(w_q_up, w_kv_up):
    depth = w_q_up.shape[0]
    qd = C_NOPE + C_ROPE
    wq = jnp.pad(w_q_up.reshape(depth, C_Q_LORA, C_HEADS, qd), ((0, 0), (0, 0), (0, 0), (0, LANES - qd)))
    kv = w_kv_up.reshape(depth, C_KV_LORA, C_HEADS, C_NOPE + C_VDIM)
    wkn = jnp.pad(kv[..., :C_NOPE], ((0, 0), (0, 0), (0, 0), (0, LANES - C_NOPE)))
    wv = jnp.pad(kv[..., C_NOPE:], ((0, 0), (0, 0), (0, 0), (0, LANES - C_VDIM)))
    flat = lambda w: w.reshape(depth, w.shape[1], C_HEADS * LANES).astype(BF16)
    return flat(wq), flat(wkn), flat(wv)


def _relayout_w_branch(w_branch):
    depth = w_branch.shape[0]
    wba = w_branch[:, 0]
    b = w_branch[:, 1].reshape(depth, B_HEADS, HD, D)
    zb = jnp.zeros_like(b)
    lo = jnp.concatenate([b, zb], axis=2)
    hi = jnp.concatenate([zb, b], axis=2)
    grp = (jnp.arange(B_HEADS) // B_GROUP).reshape(1, B_HEADS, 1, 1)
    wbb = jnp.where(grp == 0, lo, hi).reshape(depth, B_HEADS * LANES, D)
    c = w_branch[:, 2].reshape(depth, C_HEADS, C_VDIM, D)
    wbc = jnp.pad(c, ((0, 0), (0, 0), (0, LANES - C_VDIM), (0, 0))).reshape(depth, C_HEADS * LANES, D)
    return wba.astype(BF16), wbb.astype(BF16), wbc.astype(BF16)


def _rope_tables(n_tokens, rot_dim, lane_off, group):
    rows_n = n_tokens // GRID_W
    row, col = jnp.meshgrid(jnp.arange(rows_n), jnp.arange(GRID_W), indexing="ij")
    row = row.reshape(-1).astype(F32)
    col = col.reshape(-1).astype(F32)
    n_freq = rot_dim // 4
    inv = jnp.power(ROPE_THETA, -jnp.arange(n_freq, dtype=F32) / n_freq)
    ang = jnp.concatenate([row[:, None] * inv, col[:, None] * inv], axis=-1)
    cos, sin = jnp.cos(ang), jnp.sin(ang)
    half = rot_dim // 2
    c = jnp.ones((n_tokens, LANES), F32)
    sa = jnp.zeros((n_tokens, LANES), F32)
    sb = jnp.zeros((n_tokens, LANES), F32)
    for off in range(lane_off, LANES - rot_dim + 1, group):
        c = c.at[:, off:off + half].set(cos).at[:, off + half:off + rot_dim].set(cos)
        sa = sa.at[:, off:off + half].set(-sin)
        sb = sb.at[:, off + half:off + rot_dim].set(sin)
        if group >= LANES:
            break
    ident = (jnp.ones((TM, LANES), F32), jnp.zeros((TM, LANES), F32), jnp.zeros((TM, LANES), F32))
    return tuple(jnp.concatenate([a, i], axis=0) for a, i in zip((c, sa, sb), ident))


def _run_group(x, nb, s, row_of_tile, tab_block, tabs, caches, mod4, w):
    depth = w["w_a"].shape[0]
    t = nb * s
    assert t % TM == 0 and (s % TM == 0 or TM % s == 0)
    pas, ckvns = [], []
    for l in range(depth):
        pa = _proj(x, mod4, w["norm_attn_g"], w["w_a"], l, row_of_tile)
        qa, ka, va, qb, kb, vb, qc, kc, vc, ckvn = _prep(
            pa, tabs, w["gq"], w["gkv"], w["wq"], w["wkn"], w["wv"], l, tab_block)
        if caches is None:
            ctx_a = ctx_b = ctx_c = None
            col = lambda ch, n, off=0: pa[:, ch * LANES + off:ch * LANES + off + n]
            pas.append((col(CH_AK, A_HEADS * LANES), col(CH_AV, A_HEADS * LANES), col(CH_BK, LANES),
                        col(CH_BV, LANES), col(CH_CKR, C_ROPE, KR_OFF)))
            ckvns.append(ckvn)
        else:
            ctx_a, ctx_b, ctx_c = caches
        oa = _attn_a(qa, ka, va, ctx_a, w["lam4"], w["subg"], l, nb, s)
        ob = _attn_b(qb, kb, vb, ctx_b, w["win_sink"], l, nb, s)
        oc = _attn_c(qc, kc, vc, ctx_c, w["wkn"], w["wv"], l, nb, s)
        x = _merge(x, mod4, w["norm_attn_g"], w["w_gates"], oa, ob, oc,
                   w["wba"], w["wbb"], w["wbc"], w["w_out"], l, row_of_tile)
        if l % 2 == 0:
            x = _ffn(x, mod4, w["norm_ffn_g"], w["ffn_g"], w["ffn_u"], w["ffn_d"], l, l // 2, row_of_tile)
        else:
            x = _moe(x, mod4, w["norm_ffn_g"], w["moe_r"], w["moe_g"], w["moe_u"], w["moe_d"],
                     l, l // 2, row_of_tile)
    return _final_norm(x, w["final_g"]), pas, ckvns


def kernel(x_prompt, x_sample, cache_diff_k, cache_diff_v, cache_win_k, cache_win_v, cache_mla_ckv, cache_mla_krope, c, c_ctx, w_ada, b_ada, norm_attn_g, norm_ffn_g, w_in, diff_lambda_q1, diff_lambda_k1, diff_lambda_q2, diff_lambda_k2, diff_subln_g, win_sink, mla_q_norm_g, mla_w_q_up, mla_kv_norm_g, mla_w_kv_up, w_branch, w_out, ffn_w_gate, ffn_w_up, ffn_w_down, moe_w_router, moe_w_gate, moe_w_up, moe_w_down, final_norm_g):
    depth = w_in.shape[0]
    nbc, sc, _ = x_prompt.shape
    nbl, sl, _ = x_sample.shape
    past = cache_diff_k.shape[2]
    assert 1 + nbl <= MOD_ROWS

    cond = jnp.zeros((MOD_ROWS, D), F32).at[0].set(c_ctx).at[1:1 + nbl].set(c)
    mod4 = _ada_mod(cond, w_ada, b_ada).reshape(depth, MOD_ROWS, 1, 6 * D)

    w_a, w_gates = _relayout_w_in(w_in)
    wq, wkn, wv = _relayout_mla(mla_w_q_up, mla_w_kv_up)
    wba, wbb, wbc = _relayout_w_branch(w_branch)
    vec3 = lambda a: a.reshape(a.shape[0], 1, a.shape[1])
    w = dict(
        w_a=w_a, w_gates=w_gates, wq=wq, wkn=wkn, wv=wv, wba=wba, wbb=wbb, wbc=wbc,
        w_out=w_out.astype(BF16), norm_attn_g=vec3(norm_attn_g), norm_ffn_g=vec3(norm_ffn_g),
        gq=vec3(mla_q_norm_g), gkv=vec3(mla_kv_norm_g), subg=vec3(diff_subln_g),
        lam4=[vec3(a) for a in (diff_lambda_q1, diff_lambda_k1, diff_lambda_q2, diff_lambda_k2)],
        win_sink=win_sink,
        ffn_g=ffn_w_gate.astype(BF16), ffn_u=ffn_w_up.astype(BF16), ffn_d=ffn_w_down.astype(BF16),
        moe_r=jnp.pad(moe_w_router, ((0, 0), (0, 0), (0, LANES - N_EXPERTS))),
        moe_g=moe_w_gate.astype(BF16), moe_u=moe_w_up.astype(BF16), moe_d=moe_w_down.astype(BF16),
        final_g=final_norm_g.reshape(1, D),
    )

    tabs_h = _rope_tables(sl, HD, 0, HD)
    tabs_r = _rope_tables(sl, C_ROPE, KR_OFF, LANES)
    tabs = tabs_h + tabs_r
    ident_block = sl // TM

    y_c, pas, ckvns = _run_group(
        x_prompt.reshape(nbc * sc, D), nbc, sc, lambda i: 0, lambda i: ident_block, tabs, None, mod4, w)

    tiles_per_b = sl // TM
    krp = jnp.pad(cache_mla_krope, ((0, 0), (0, 0), (0, 0), (KR_OFF, LANES - KR_OFF - C_ROPE)))
    caches = (
        (cache_diff_k.reshape(nbl, depth, past, A_HEADS * LANES),
         cache_diff_v.reshape(nbl, depth, past, A_HEADS * LANES)),
        (cache_win_k.reshape(nbl, depth, past, LANES), cache_win_v.reshape(nbl, depth, past, LANES)),
        (cache_mla_ckv, krp),
    )
    y_l, _, _ = _run_group(
        x_sample.reshape(nbl * sl, D), nbl, sl, lambda i: 1 + i // tiles_per_b,
        lambda i: i % tiles_per_b, tabs, caches, mod4, w)

    def cache(per_layer, tail):
        return jnp.stack([a.reshape((nbc, sc) + tail) for a in per_layer], axis=1)

    new_diff_k = cache([p[0] for p in pas], (A_HEADS, 2 * HD))
    new_diff_v = cache([p[1] for p in pas], (A_HEADS, 2 * HD))
    new_win_k = cache([p[2] for p in pas], (B_HEADS // B_GROUP, HD))
    new_win_v = cache([p[3] for p in pas], (B_HEADS // B_GROUP, HD))
    new_mla_ckv = cache(ckvns, (C_KV_LORA,))
    new_mla_krope = cache([p[4] for p in pas], (C_ROPE,))
    return (y_c.reshape(nbc, sc, D), y_l.reshape(nbl, sl, D), new_diff_k, new_diff_v,
            new_win_k, new_win_v, new_mla_ckv, new_mla_krope)
```

```python
import functools
import math

import jax
import jax.numpy as jnp
from jax import lax
from jax.experimental import pallas as pl
from jax.experimental.pallas import tpu as pltpu

F32 = jnp.float32
BF16 = jnp.bfloat16

D = 1024
HD = 64
LANES = 128
GRID_W = 64
BLOCK = 128
WINDOW = 128
A_HEADS = 4
B_HEADS = 8
B_GROUP = 4
C_HEADS = 8
C_Q_LORA = 256
C_KV_LORA = 128
C_NOPE = 64
C_ROPE = 32
C_VDIM = 64
N_EXPERTS = 8
ROPE_THETA = 10000.0
EPS = 1e-6
NEG_INF = -1e30
LOG2E = math.log2(math.e)
MOD_ROWS = 16

TM = 512
TQ = 256
TQB = 512
VMEM_LIMIT = 56 * 1024 * 1024

CH_AQ, CH_AK, CH_AV, CH_BQ, CH_BK, CH_BV, CH_CQ, CH_CKV, CH_CKR = 0, 4, 8, 12, 20, 21, 22, 24, 25
N_CH = 26
NPA = N_CH * LANES
KR_OFF = C_NOPE


def _cparams(sem):
    return pltpu.CompilerParams(dimension_semantics=sem, vmem_limit_bytes=VMEM_LIMIT)


def _rms(x, g):
    return x * lax.rsqrt(jnp.mean(x * x, axis=-1, keepdims=True) + EPS) * g


def _sigmoid(x):
    return 1.0 / (1.0 + jnp.exp(-x))


def _dot(a, b):
    return jnp.dot(a, b, preferred_element_type=F32)


def _dot_t(a, b):
    return lax.dot_general(a, b, (((1,), (1,)), ((), ())), preferred_element_type=F32)


def _ada_kernel(c_ref, w_ref, b_ref, o_ref):
    c = c_ref[...]
    s = (c * _sigmoid(c)).astype(BF16)
    o_ref[...] = _dot(s, w_ref[...].astype(BF16)) + b_ref[...]


def _ada_mod(cond, w_ada, b_ada):
    depth, _, n = w_ada.shape
    tn = 1536
    return pl.pallas_call(
        _ada_kernel,
        out_shape=jax.ShapeDtypeStruct((depth, MOD_ROWS, n), F32),
        grid=(depth, n // tn),
        in_specs=[
            pl.BlockSpec((MOD_ROWS, D), lambda l, j: (0, 0)),
            pl.BlockSpec((None, D, tn), lambda l, j: (l, 0, j)),
            pl.BlockSpec((None, 1, tn), lambda l, j: (l, 0, j)),
        ],
        out_specs=pl.BlockSpec((None, MOD_ROWS, tn), lambda l, j: (l, 0, j)),
        compiler_params=_cparams(("parallel", "parallel")),
        name="ada_mod",
    )(cond, w_ada, b_ada.reshape(depth, 1, n))


def _mod_spec(l, k, row_of_tile):
    return pl.BlockSpec((None, None, 1, D), lambda i: (l, row_of_tile(i), 0, k))


def _norm_mod(x, g_ref, shift_ref, scale_ref):
    return _rms(x, g_ref[...]) * (1.0 + scale_ref[...]) + shift_ref[...]


def _proj_kernel(x_ref, shift_ref, scale_ref, g_ref, w_ref, o_ref):
    h = _norm_mod(x_ref[...], g_ref, shift_ref, scale_ref).astype(BF16)
    o_ref[...] = _dot(h, w_ref[...])


def _proj(x, mod4, norm_g3, w_a, l, row_of_tile):
    t = x.shape[0]
    return pl.pallas_call(
        _proj_kernel,
        out_shape=jax.ShapeDtypeStruct((t, NPA), F32),
        grid=(t // TM,),
        in_specs=[
            pl.BlockSpec((TM, D), lambda i: (i, 0)),
            _mod_spec(l, 0, row_of_tile),
            _mod_spec(l, 1, row_of_tile),
            pl.BlockSpec((None, 1, D), lambda i: (l, 0, 0)),
            pl.BlockSpec((None, D, NPA), lambda i: (l, 0, 0)),
        ],
        out_specs=pl.BlockSpec((TM, NPA), lambda i: (i, 0)),
        compiler_params=_cparams(("parallel",)),
        name="proj",
    )(x, mod4, mod4, norm_g3, w_a)


def _rope(x, c, sa, sb, half):
    return x * c + pltpu.roll(x, LANES - half, 1) * sa + pltpu.roll(x, half, 1) * sb


def _prep_kernel(pa_ref, hc_ref, hsa_ref, hsb_ref, rc_ref, rsa_ref, rsb_ref, gq_ref, gkv_ref,
                 wq_ref, wkn_ref, wv_ref,
                 qa_ref, ka_ref, va_ref, qb_ref, kb_ref, vb_ref, qc_ref, kc_ref, vc_ref, ckvn_ref):
    hc, hsa, hsb = hc_ref[...], hsa_ref[...], hsb_ref[...]
    rc, rsa, rsb = rc_ref[...], rsa_ref[...], rsb_ref[...]

    def chunk(c):
        return pa_ref[:, c * LANES:(c + 1) * LANES]

    def put(ref, c, val):
        ref[:, c * LANES:(c + 1) * LANES] = val.astype(ref.dtype)

    q_scale = HD ** -0.5
    for c in range(A_HEADS):
        put(qa_ref, c, _rope(chunk(CH_AQ + c), hc, hsa, hsb, HD // 2) * (q_scale * LOG2E))
        put(ka_ref, c, _rope(chunk(CH_AK + c), hc, hsa, hsb, HD // 2))
        put(va_ref, c, chunk(CH_AV + c))
    for c in range(B_HEADS):
        put(qb_ref, c, _rope(chunk(CH_BQ + c), hc, hsa, hsb, HD // 2) * q_scale)
    put(kb_ref, 0, _rope(chunk(CH_BK), hc, hsa, hsb, HD // 2))
    put(vb_ref, 0, chunk(CH_BV))

    cq = pa_ref[:, CH_CQ * LANES:(CH_CQ + 2) * LANES]
    cqn = _rms(cq, gq_ref[...]).astype(BF16)
    qup = _dot(cqn, wq_ref[...])
    c_scale = (C_NOPE + C_ROPE) ** -0.5 * LOG2E
    for c in range(C_HEADS):
        put(qc_ref, c, _rope(qup[:, c * LANES:(c + 1) * LANES], rc, rsa, rsb, C_ROPE // 2) * c_scale)

    ckvn = _rms(chunk(CH_CKV), gkv_ref[...])
    ckvn_ref[...] = ckvn
    ckvb = ckvn.astype(BF16)
    kn = _dot(ckvb, wkn_ref[...])
    kr = _rope(chunk(CH_CKR), rc, rsa, rsb, C_ROPE // 2)
    for c in range(C_HEADS):
        put(kc_ref, c, kn[:, c * LANES:(c + 1) * LANES] + kr)
    vc_ref[...] = _with_ones_lane(_dot(ckvb, wv_ref[...])).astype(BF16)


def _prep(pa, tabs, gq3, gkv3, wq, wkn, wv, l, tab_block):
    t = pa.shape[0]
    tab_spec = pl.BlockSpec((TM, LANES), lambda i: (tab_block(i), 0))
    wide = lambda n: pl.BlockSpec((TM, n), lambda i: (i, 0))
    outs = [(4 * LANES, BF16)] * 3 + [(8 * LANES, BF16), (LANES, BF16), (LANES, BF16)] \
        + [(8 * LANES, BF16)] * 3 + [(LANES, F32)]
    return pl.pallas_call(
        _prep_kernel,
        out_shape=[jax.ShapeDtypeStruct((t, n), dt) for n, dt in outs],
        grid=(t // TM,),
        in_specs=[wide(NPA)] + [tab_spec] * 6 + [
            pl.BlockSpec((None, 1, C_Q_LORA), lambda i: (l, 0, 0)),
            pl.BlockSpec((None, 1, C_KV_LORA), lambda i: (l, 0, 0)),
            pl.BlockSpec((None, C_Q_LORA, C_HEADS * LANES), lambda i: (l, 0, 0)),
            pl.BlockSpec((None, C_KV_LORA, C_HEADS * LANES), lambda i: (l, 0, 0)),
            pl.BlockSpec((None, C_KV_LORA, C_HEADS * LANES), lambda i: (l, 0, 0)),
        ],
        out_specs=[wide(n) for n, _ in outs],
        compiler_params=_cparams(("parallel",)),
        name="prep",
    )(pa, *tabs, gq3, gkv3, wq, wkn, wv)


def _exp2_parts(scores):
    m = functools.reduce(jnp.maximum, [jnp.max(s, axis=-1, keepdims=True) for s in scores])
    es = [jnp.exp2(s - m) for s in scores]
    den = functools.reduce(lambda a, b: a + b, [jnp.sum(e, axis=-1, keepdims=True) for e in es])
    return es, den


def _attn_a_kernel(lam_init, has_ctx, q_ref, k_ref, v_ref, *rest):
    if has_ctx:
        kc_ref, vc_ref, lq1_ref, lk1_ref, lq2_ref, lk2_ref, sg_ref, o_ref = rest
    else:
        lq1_ref, lk1_ref, lq2_ref, lk2_ref, sg_ref, o_ref = rest
    lam = (jnp.exp(jnp.sum(lq1_ref[...] * lk1_ref[...], axis=-1, keepdims=True))
           - jnp.exp(jnp.sum(lq2_ref[...] * lk2_ref[...], axis=-1, keepdims=True)) + lam_init)
    lane = lax.broadcasted_iota(jnp.int32, (q_ref.shape[0], LANES), 1)
    for h in range(A_HEADS):
        sl = slice(h * LANES, (h + 1) * LANES)
        q = q_ref[:, sl]
        zero = jnp.zeros_like(q)
        q1 = jnp.where(lane < HD, q, zero)
        q2 = jnp.where(lane >= HD, q, zero)
        keys, vals = [k_ref[:, sl]], [v_ref[:, sl]]
        if has_ctx:
            keys.append(kc_ref[:, sl].astype(BF16))
            vals.append(vc_ref[:, sl].astype(BF16))
        e1, l1 = _exp2_parts([_dot_t(q1, k) for k in keys])
        e2, l2 = _exp2_parts([_dot_t(q2, k) for k in keys])
        c = lam * l1 / l2
        o = None
        for a1, a2, v in zip(e1, e2, vals):
            part = _dot((a1 - c * a2).astype(BF16), v)
            o = part if o is None else o + part
        o = o * (1.0 / l1)
        o_ref[:, sl] = (_rms(o, sg_ref[...]) * (1.0 - lam_init)).astype(o_ref.dtype)


def _attn_a(qa, ka, va, ctx, lam4, subg3, l, nb, s):
    t = qa.shape[0]
    tq = min(TQ, s)
    nq = s // tq
    w = A_HEADS * LANES
    lam_init = 0.8 - 0.6 * math.exp(-0.3 * l)
    in_specs = [
        pl.BlockSpec((tq, w), lambda b, i: (b * nq + i, 0)),
        pl.BlockSpec((s, w), lambda b, i: (b, 0)),
        pl.BlockSpec((s, w), lambda b, i: (b, 0)),
    ]
    args = [qa, ka, va]
    if ctx is not None:
        past = ctx[0].shape[2]
        in_specs += [pl.BlockSpec((None, None, past, w), lambda b, i: (b, l, 0, 0))] * 2
        args += list(ctx)
    in_specs += [pl.BlockSpec((None, 1, HD), lambda b, i: (l, 0, 0))] * 4
    in_specs += [pl.BlockSpec((None, 1, LANES), lambda b, i: (l, 0, 0))]
    return pl.pallas_call(
        functools.partial(_attn_a_kernel, lam_init, ctx is not None),
        out_shape=jax.ShapeDtypeStruct((t, w), BF16),
        grid=(nb, nq),
        in_specs=in_specs,
        out_specs=pl.BlockSpec((tq, w), lambda b, i: (b * nq + i, 0)),
        compiler_params=_cparams(("parallel", "arbitrary")),
        name="attn_a_lat" if ctx is not None else "attn_a_ctx",
    )(*args, *lam4, subg3)


def _sink_attend(qs, keys, vals, masks, sinkv):
    scores = []
    for k, mask in zip(keys, masks):
        s = _dot_t(qs, k)
        scores.append(s if mask is None else jnp.where(mask, s, NEG_INF))
    m = functools.reduce(jnp.maximum, [jnp.max(s, axis=-1, keepdims=True) for s in scores] + [sinkv])
    es = [jnp.exp(s - m) for s in scores]
    den = functools.reduce(lambda a, b: a + b,
                           [jnp.sum(e, axis=-1, keepdims=True) for e in es] + [jnp.exp(sinkv - m)])
    r = 1.0 / den
    o = None
    for e, v in zip(es, vals):
        part = _dot((e * r).astype(BF16), v)
        o = part if o is None else o + part
    return o


def _stack_group(q_ref, row0, rows, g):
    return jnp.concatenate(
        [q_ref[pl.ds(row0, rows), (B_GROUP * g + j) * LANES:(B_GROUP * g + j + 1) * LANES]
         for j in range(B_GROUP)], axis=0)


def _sink_column(sink_ref, l, g, rows):
    row = lax.broadcasted_iota(jnp.int32, (B_GROUP * rows, 1), 0)
    col = jnp.full((B_GROUP * rows, 1), sink_ref[l, B_GROUP * g + B_GROUP - 1], F32)
    for j in range(B_GROUP - 2, -1, -1):
        col = jnp.where(row < (j + 1) * rows, sink_ref[l, B_GROUP * g + j], col)
    return col


def _store_group(o_ref, row0, rows, g, o):
    lane = lax.broadcasted_iota(jnp.int32, o.shape, 1)
    o = jnp.where((lane >= g * HD) & (lane < (g + 1) * HD), o, 0.0).astype(o_ref.dtype)
    for j in range(B_GROUP):
        c = B_GROUP * g + j
        o_ref[pl.ds(row0, rows), c * LANES:(c + 1) * LANES] = o[j * rows:(j + 1) * rows, :]


def _attn_b_ctx_kernel(l, sink_ref, q_ref, k_ref, v_ref, o_ref):
    rows = q_ref.shape[0]
    k, v = k_ref[...], v_ref[...]
    for g in range(B_HEADS // B_GROUP):
        qs = _stack_group(q_ref, 0, rows, g)
        o = _sink_attend(qs, [k], [v], [None], _sink_column(sink_ref, l, g, rows))
        _store_group(o_ref, 0, rows, g, o)


def _attn_b_lat_kernel(l, s_len, sink_ref, q_ref, k_ref, v_ref, kc_ref, vc_ref, o_ref):
    qi = pl.program_id(1)
    kc = kc_ref[...].astype(BF16)
    vc = vc_ref[...].astype(BF16)
    win = 3 * BLOCK
    for n in range(TQB // BLOCK):
        blk = qi * (TQB // BLOCK) + n
        ws = pl.multiple_of(jnp.clip((blk - 1) * BLOCK, 0, s_len - win), BLOCK)
        kw = k_ref[pl.ds(ws, win), :]
        vw = v_ref[pl.ds(ws, win), :]
        qpos = blk * BLOCK + lax.broadcasted_iota(jnp.int32, (B_GROUP * BLOCK, win), 0) % BLOCK
        kpos = ws + lax.broadcasted_iota(jnp.int32, (B_GROUP * BLOCK, win), 1)
        mask = jnp.abs(qpos - kpos) <= WINDOW
        for g in range(B_HEADS // B_GROUP):
            qs = _stack_group(q_ref, n * BLOCK, BLOCK, g)
            o = _sink_attend(qs, [kw, kc], [vw, vc], [mask, None], _sink_column(sink_ref, l, g, BLOCK))
            _store_group(o_ref, n * BLOCK, BLOCK, g, o)


def _attn_b(qb, kb, vb, ctx, win_sink, l, nb, s):
    t = qb.shape[0]
    smem = pl.BlockSpec(memory_space=pltpu.SMEM)
    if ctx is None:
        return pl.pallas_call(
            functools.partial(_attn_b_ctx_kernel, l),
            out_shape=jax.ShapeDtypeStruct((t, B_HEADS * LANES), BF16),
            grid=(nb,),
            in_specs=[smem,
                      pl.BlockSpec((s, B_HEADS * LANES), lambda b: (b, 0)),
                      pl.BlockSpec((s, LANES), lambda b: (b, 0)),
                      pl.BlockSpec((s, LANES), lambda b: (b, 0))],
            out_specs=pl.BlockSpec((s, B_HEADS * LANES), lambda b: (b, 0)),
            compiler_params=_cparams(("parallel",)),
            name="attn_b_ctx",
        )(win_sink, qb, kb, vb)
    assert s % TQB == 0 and s >= 3 * BLOCK
    nq = s // TQB
    past = ctx[0].shape[2]
    return pl.pallas_call(
        functools.partial(_attn_b_lat_kernel, l, s),
        out_shape=jax.ShapeDtypeStruct((t, B_HEADS * LANES), BF16),
        grid=(nb, nq),
        in_specs=[smem,
                  pl.BlockSpec((TQB, B_HEADS * LANES), lambda b, i: (b * nq + i, 0)),
                  pl.BlockSpec((s, LANES), lambda b, i: (b, 0)),
                  pl.BlockSpec((s, LANES), lambda b, i: (b, 0)),
                  pl.BlockSpec((None, None, past, LANES), lambda b, i: (b, l, 0, 0)),
                  pl.BlockSpec((None, None, past, LANES), lambda b, i: (b, l, 0, 0))],
        out_specs=pl.BlockSpec((TQB, B_HEADS * LANES), lambda b, i: (b * nq + i, 0)),
        compiler_params=_cparams(("parallel", "arbitrary")),
        name="attn_b_lat",
    )(win_sink, qb, kb, vb, *ctx)


def _with_ones_lane(v):
    lane = lax.broadcasted_iota(jnp.int32, v.shape, 1)
    return jnp.where(lane % LANES == C_VDIM, 1.0, v)


def _attn_c_kernel(has_ctx, q_ref, k_ref, v_ref, *rest):
    if has_ctx:
        ckv_ref, krp_ref, wkn_ref, wv_ref, o_ref, kc_s, vc_s = rest

        @pl.when(pl.program_id(1) == 0)
        def _():
            ckv = ckv_ref[...].astype(BF16)
            kn = _dot(ckv, wkn_ref[...])
            krp = krp_ref[...]
            for h in range(C_HEADS):
                sl = slice(h * LANES, (h + 1) * LANES)
                kc_s[:, sl] = (kn[:, sl] + krp).astype(BF16)
            vc_s[...] = _with_ones_lane(_dot(ckv, wv_ref[...])).astype(BF16)
    else:
        (o_ref,) = rest
    for h in range(C_HEADS):
        sl = slice(h * LANES, (h + 1) * LANES)
        q = q_ref[:, sl]
        keys, vals = [k_ref[:, sl]], [v_ref[:, sl]]
        if has_ctx:
            keys.append(kc_s[:, sl])
            vals.append(vc_s[:, sl])
        scores = [_dot_t(q, k) for k in keys]
        m = functools.reduce(jnp.maximum, [jnp.max(s, axis=-1, keepdims=True) for s in scores])
        o = None
        for s, v in zip(scores, vals):
            part = _dot(jnp.exp2(s - m).astype(BF16), v)
            o = part if o is None else o + part
        o_ref[:, sl] = (o * (1.0 / o[:, C_VDIM:C_VDIM + 1])).astype(o_ref.dtype)


def _attn_c(qc, kc, vc, ctx, wkn, wv, l, nb, s):
    t = qc.shape[0]
    tq = min(TQ, s)
    nq = s // tq
    w = C_HEADS * LANES
    in_specs = [
        pl.BlockSpec((tq, w), lambda b, i: (b * nq + i, 0)),
        pl.BlockSpec((s, w), lambda b, i: (b, 0)),
        pl.BlockSpec((s, w), lambda b, i: (b, 0)),
    ]
    args = [qc, kc, vc]
    scratch = []
    if ctx is not None:
        past = ctx[0].shape[2]
        in_specs += [pl.BlockSpec((None, None, past, LANES), lambda b, i: (b, l, 0, 0))] * 2
        in_specs += [pl.BlockSpec((None, C_KV_LORA, w), lambda b, i: (l, 0, 0))] * 2
        args += list(ctx) + [wkn, wv]
        scratch = [pltpu.VMEM((past, w), BF16)] * 2
    return pl.pallas_call(
        functools.partial(_attn_c_kernel, ctx is not None),
        out_shape=jax.ShapeDtypeStruct((t, w), BF16),
        grid=(nb, nq),
        in_specs=in_specs,
        out_specs=pl.BlockSpec((tq, w), lambda b, i: (b * nq + i, 0)),
        scratch_shapes=scratch,
        compiler_params=_cparams(("parallel", "arbitrary")),
        name="attn_c_lat" if ctx is not None else "attn_c_ctx",
    )(*args)


def _merge_kernel(x_ref, shift_ref, scale_ref, gate_ref, g_ref, wg_ref, oa_ref, ob_ref, oc_ref,
                  wba_ref, wbb_ref, wbc_ref, wo_ref, o_ref):
    x = x_ref[...]
    h = _norm_mod(x, g_ref, shift_ref, scale_ref).astype(BF16)
    mix = None
    for n, (br_ref, wb_ref) in enumerate(((oa_ref, wba_ref), (ob_ref, wbb_ref), (oc_ref, wbc_ref))):
        gate = _sigmoid(_dot(h, wg_ref[:, n * D:(n + 1) * D]))
        term = gate * _dot(br_ref[...], wb_ref[...])
        mix = term if mix is None else mix + term
    o_ref[...] = x + gate_ref[...] * _dot(mix.astype(BF16), wo_ref[...])


def _merge(x, mod4, norm_g3, wg, oa, ob, oc, wba, wbb, wbc, wo, l, row_of_tile):
    t = x.shape[0]
    row = lambda n: pl.BlockSpec((TM, n), lambda i: (i, 0))
    res = lambda r, c: pl.BlockSpec((None, r, c), lambda i: (l, 0, 0))
    return pl.pallas_call(
        _merge_kernel,
        out_shape=jax.ShapeDtypeStruct((t, D), F32),
        grid=(t // TM,),
        in_specs=[row(D), _mod_spec(l, 0, row_of_tile), _mod_spec(l, 1, row_of_tile),
                  _mod_spec(l, 2, row_of_tile), res(1, D), res(D, 3 * D),
                  row(A_HEADS * LANES), row(B_HEADS * LANES), row(C_HEADS * LANES),
                  res(A_HEADS * LANES, D), res(B_HEADS * LANES, D), res(C_HEADS * LANES, D), res(D, D)],
        out_specs=row(D),
        compiler_params=_cparams(("parallel",)),
        name="merge",
    )(x, mod4, mod4, mod4, norm_g3, wg, oa, ob, oc, wba, wbb, wbc, wo)


def _swiglu(h, wg, wu):
    a = _dot(h, wg)
    return (a * _sigmoid(a) * _dot(h, wu)).astype(BF16)


def _ffn_kernel(tf, x_ref, shift_ref, scale_ref, gate_ref, g_ref, wg_ref, wu_ref, wd_ref, o_ref, acc_ref):
    x = x_ref[...]
    h = _norm_mod(x, g_ref, shift_ref, scale_ref).astype(BF16)
    acc_ref[...] = jnp.zeros_like(acc_ref)

    def body(c, carry):
        off = pl.multiple_of(c * tf, tf)
        a = _swiglu(h, wg_ref[:, pl.ds(off, tf)], wu_ref[:, pl.ds(off, tf)])
        acc_ref[...] += _dot(a, wd_ref[pl.ds(off, tf), :])
        return carry

    lax.fori_loop(0, wg_ref.shape[1] // tf, body, 0)
    o_ref[...] = x + gate_ref[...] * acc_ref[...]


def _ffn(x, mod4, norm_g3, wg, wu, wd, l, idx, row_of_tile):
    t = x.shape[0]
    dff = wg.shape[2]
    tf = 256
    assert dff % tf == 0
    row = pl.BlockSpec((TM, D), lambda i: (i, 0))
    return pl.pallas_call(
        functools.partial(_ffn_kernel, tf),
        out_shape=jax.ShapeDtypeStruct((t, D), F32),
        grid=(t // TM,),
        in_specs=[row, _mod_spec(l, 3, row_of_tile), _mod_spec(l, 4, row_of_tile),
                  _mod_spec(l, 5, row_of_tile),
                  pl.BlockSpec((None, 1, D), lambda i: (l, 0, 0)),
                  pl.BlockSpec((None, D, dff), lambda i: (idx, 0, 0)),
                  pl.BlockSpec((None, D, dff), lambda i: (idx, 0, 0)),
                  pl.BlockSpec((None, dff, D), lambda i: (idx, 0, 0))],
        out_specs=row,
        scratch_shapes=[pltpu.VMEM((TM, D), F32)],
        compiler_params=_cparams(("parallel",)),
        name="ffn",
    )(x, mod4, mod4, mod4, norm_g3, wg, wu, wd)


def _top2_gate(logits):
    lane = lax.broadcasted_iota(jnp.int32, logits.shape, 1)
    valid = lane < N_EXPERTS
    lg = jnp.where(valid, logits, NEG_INF)
    m1 = jnp.max(lg, axis=-1, keepdims=True)
    i1 = jnp.min(jnp.where(lg == m1, lane, LANES), axis=-1, keepdims=True)
    rest = jnp.where(lane == i1, NEG_INF, lg)
    m2 = jnp.max(rest, axis=-1, keepdims=True)
    i2 = jnp.min(jnp.where(rest == m2, lane, LANES), axis=-1, keepdims=True)
    e2 = jnp.exp(m2 - m1)
    den = 1.0 + e2
    return jnp.where(lane == i1, 1.0 / den, jnp.where(lane == i2, e2 / den, 0.0))


def _moe_kernel(x_ref, shift_ref, scale_ref, gate_ref, g_ref, wr_ref, wg_ref, wu_ref, wd_ref, o_ref,
                h_s, gate_s, acc_s):
    e = pl.program_id(1)

    @pl.when(e == 0)
    def _():
        h = _norm_mod(x_ref[...], g_ref, shift_ref, scale_ref)
        h_s[...] = h.astype(BF16)
        logits = jnp.dot(h, wr_ref[...], preferred_element_type=F32, precision=lax.Precision.HIGHEST)
        gate_s[...] = _top2_gate(logits)
        acc_s[...] = jnp.zeros_like(acc_s)

    h = h_s[...]
    y = _dot(_swiglu(h, wg_ref[...], wu_ref[...]), wd_ref[...])
    lane = lax.broadcasted_iota(jnp.int32, gate_s.shape, 1)
    ge = jnp.sum(jnp.where(lane == e, gate_s[...], 0.0), axis=-1, keepdims=True)
    acc_s[...] += ge * y

    @pl.when(e == N_EXPERTS - 1)
    def _():
        o_ref[...] = x_ref[...] + gate_ref[...] * acc_s[...]


def _moe(x, mod4, norm_g3, wr, wg, wu, wd, l, idx, row_of_tile):
    t = x.shape[0]
    dfe = wg.shape[3]
    row = pl.BlockSpec((TM, D), lambda i, e: (i, 0))
    mod = lambda k: pl.BlockSpec((None, None, 1, D), lambda i, e: (l, row_of_tile(i), 0, k))
    return pl.pallas_call(
        _moe_kernel,
        out_shape=jax.ShapeDtypeStruct((t, D), F32),
        grid=(t // TM, N_EXPERTS),
        in_specs=[row, mod(3), mod(4), mod(5),
                  pl.BlockSpec((None, 1, D), lambda i, e: (l, 0, 0)),
                  pl.BlockSpec((None, D, LANES), lambda i, e: (idx, 0, 0)),
                  pl.BlockSpec((None, None, D, dfe), lambda i, e: (idx, e, 0, 0)),
                  pl.BlockSpec((None, None, D, dfe), lambda i, e: (idx, e, 0, 0)),
                  pl.BlockSpec((None, None, dfe, D), lambda i, e: (idx, e, 0, 0))],
        out_specs=row,
        scratch_shapes=[pltpu.VMEM((TM, D), BF16), pltpu.VMEM((TM, LANES), F32), pltpu.VMEM((TM, D), F32)],
        compiler_params=_cparams(("parallel", "arbitrary")),
        name="moe",
    )(x, mod4, mod4, mod4, norm_g3, wr, wg, wu, wd)


def _final_kernel(x_ref, g_ref, o_ref):
    o_ref[...] = _rms(x_ref[...], g_ref[...])


def _final_norm(x, g2):
    t = x.shape[0]
    return pl.pallas_call(
        _final_kernel,
        out_shape=jax.ShapeDtypeStruct((t, D), F32),
        grid=(t // TM,),
        in_specs=[pl.BlockSpec((TM, D), lambda i: (i, 0)), pl.BlockSpec((1, D), lambda i: (0, 0))],
        out_specs=pl.BlockSpec((TM, D), lambda i: (i, 0)),
        compiler_params=_cparams(("parallel",)),
        name="final_norm",
    )(x, g2)


def _relayout_w_in(w_in):
    depth = w_in.shape[0]
    z = lambda n: jnp.zeros((depth, D, n), w_in.dtype)
    aq, ak, av = w_in[..., 0:512], w_in[..., 512:1024], w_in[..., 1024:1536]
    bq, bk, bv = w_in[..., 1536:2048], w_in[..., 2048:2176], w_in[..., 2176:2304]
    cq, ckv, ckr = w_in[..., 2304:2560], w_in[..., 2560:2688], w_in[..., 2688:2720]
    gates = w_in[..., 2720:2720 + 3 * D]
    pieces = [aq, ak, av]
    for h in range(B_HEADS):
        qh = bq[..., h * HD:(h + 1) * HD]
        pieces += [qh, z(HD)] if h // B_GROUP == 0 else [z(HD), qh]
    pieces += [bk, bv, cq, ckv, z(KR_OFF), ckr, z(LANES - KR_OFF - C_ROPE)]
    w_a = jnp.concatenate(pieces, axis=-1).astype(BF16)
    assert w_a.shape[-1] == NPA
    return w_a, gates.astype(BF16)


def _relayout_mla(w_q_up, w_kv_up):
    depth = w_q_up.shape[0]
    qd = C_NOPE + C_ROPE
    wq = jnp.pad(w_q_up.reshape(depth, C_Q_LORA, C_HEADS, qd), ((0, 0), (0, 0), (0, 0), (0, LANES - qd)))
    kv = w_kv_up.reshape(depth, C_KV_LORA, C_HEADS, C_NOPE + C_VDIM)
    wkn = jnp.pad(kv[..., :C_NOPE], ((0, 0), (0, 0), (0, 0), (0, LANES - C_NOPE)))
    wv = jnp.pad(kv[..., C_NOPE:], ((0, 0), (0, 0), (0, 0), (0, LANES - C_VDIM)))
    flat = lambda w: w.reshape(depth, w.shape[1], C_HEADS * LANES).astype(BF16)
    return flat(wq), flat(wkn), flat(wv)


def _relayout_w_branch(w_branch):
    depth = w_branch.shape[0]
    wba = w_branch[:, 0]
    b = w_branch[:, 1].reshape(depth, B_HEADS, HD, D)
    zb = jnp.zeros_like(b)
    lo = jnp.concatenate([b, zb], axis=2)
    hi = jnp.concatenate([zb, b], axis=2)
    grp = (jnp.arange(B_HEADS) // B_GROUP).reshape(1, B_HEADS, 1, 1)
    wbb = jnp.where(grp == 0, lo, hi).reshape(depth, B_HEADS * LANES, D)
    c = w_branch[:, 2].reshape(depth, C_HEADS, C_VDIM, D)
    wbc = jnp.pad(c, ((0, 0), (0, 0), (0, LANES - C_VDIM), (0, 0))).reshape(depth, C_HEADS * LANES, D)
    return wba.astype(BF16), wbb.astype(BF16), wbc.astype(BF16)


def _rope_tables(n_tokens, rot_dim, lane_off, group):
    rows_n = n_tokens // GRID_W
    row, col = jnp.meshgrid(jnp.arange(rows_n), jnp.arange(GRID_W), indexing="ij")
    row = row.reshape(-1).astype(F32)
    col = col.reshape(-1).astype(F32)
    n_freq = rot_dim // 4
    inv = jnp.power(ROPE_THETA, -jnp.arange(n_freq, dtype=F32) / n_freq)
    ang = jnp.concatenate([row[:, None] * inv, col[:, None] * inv], axis=-1)
    cos, sin = jnp.cos(ang), jnp.sin(ang)
    half = rot_dim // 2
    c = jnp.ones((n_tokens, LANES), F32)
    sa = jnp.zeros((n_tokens, LANES), F32)
    sb = jnp.zeros((n_tokens, LANES), F32)
    for off in range(lane_off, LANES - rot_dim + 1, group):
        c = c.at[:, off:off + half].set(cos).at[:, off + half:off + rot_dim].set(cos)
        sa = sa.at[:, off:off + half].set(-sin)
        sb = sb.at[:, off + half:off + rot_dim].set(sin)
        if group >= LANES:
            break
    ident = (jnp.ones((TM, LANES), F32), jnp.zeros((TM, LANES), F32), jnp.zeros((TM, LANES), F32))
    return tuple(jnp.concatenate([a, i], axis=0) for a, i in zip((c, sa, sb), ident))


def _run_group(x, nb, s, row_of_tile, tab_block, tabs, caches, mod4, w):
    depth = w["w_a"].shape[0]
    t = nb * s
    assert t % TM == 0 and (s % TM == 0 or TM % s == 0)
    pas, ckvns = [], []
    for l in range(depth):
        pa = _proj(x, mod4, w["norm_attn_g"], w["w_a"], l, row_of_tile)
        qa, ka, va, qb, kb, vb, qc, kc, vc, ckvn = _prep(
            pa, tabs, w["gq"], w["gkv"], w["wq"], w["wkn"], w["wv"], l, tab_block)
        if caches is None:
            ctx_a = ctx_b = ctx_c = None
            col = lambda ch, n, off=0: pa[:, ch * LANES + off:ch * LANES + off + n]
            pas.append((col(CH_AK, A_HEADS * LANES), col(CH_AV, A_HEADS * LANES), col(CH_BK, LANES),
                        col(CH_BV, LANES), col(CH_CKR, C_ROPE, KR_OFF)))
            ckvns.append(ckvn)
        else:
            ctx_a, ctx_b, ctx_c = caches
        oa = _attn_a(qa, ka, va, ctx_a, w["lam4"], w["subg"], l, nb, s)
        ob = _attn_b(qb, kb, vb, ctx_b, w["win_sink"], l, nb, s)
        oc = _attn_c(qc, kc, vc, ctx_c, w["wkn"], w["wv"], l, nb, s)
        x = _merge(x, mod4, w["norm_attn_g"], w["w_gates"], oa, ob, oc,
                   w["wba"], w["wbb"], w["wbc"], w["w_out"], l, row_of_tile)
        if l % 2 == 0:
            x = _ffn(x, mod4, w["norm_ffn_g"], w["ffn_g"], w["ffn_u"], w["ffn_d"], l, l // 2, row_of_tile)
        else:
            x = _moe(x, mod4, w["norm_ffn_g"], w["moe_r"], w["moe_g"], w["moe_u"], w["moe_d"],
                     l, l // 2, row_of_tile)
    return _final_norm(x, w["final_g"]), pas, ckvns


def kernel(x_prompt, x_sample, cache_diff_k, cache_diff_v, cache_win_k, cache_win_v, cache_mla_ckv, cache_mla_krope, c, c_ctx, w_ada, b_ada, norm_attn_g, norm_ffn_g, w_in, diff_lambda_q1, diff_lambda_k1, diff_lambda_q2, diff_lambda_k2, diff_subln_g, win_sink, mla_q_norm_g, mla_w_q_up, mla_kv_norm_g, mla_w_kv_up, w_branch, w_out, ffn_w_gate, ffn_w_up, ffn_w_down, moe_w_router, moe_w_gate, moe_w_up, moe_w_down, final_norm_g):
    depth = w_in.shape[0]
    nbc, sc, _ = x_prompt.shape
    nbl, sl, _ = x_sample.shape
    past = cache_diff_k.shape[2]
    assert 1 + nbl <= MOD_ROWS

    cond = jnp.zeros((MOD_ROWS, D), F32).at[0].set(c_ctx).at[1:1 + nbl].set(c)
    mod4 = _ada_mod(cond, w_ada, b_ada).reshape(depth, MOD_ROWS, 1, 6 * D)

    w_a, w_gates = _relayout_w_in(w_in)
    wq, wkn, wv = _relayout_mla(mla_w_q_up, mla_w_kv_up)
    wba, wbb, wbc = _relayout_w_branch(w_branch)
    vec3 = lambda a: a.reshape(a.shape[0], 1, a.shape[1])
    w = dict(
        w_a=w_a, w_gates=w_gates, wq=wq, wkn=wkn, wv=wv, wba=wba, wbb=wbb, wbc=wbc,
        w_out=w_out.astype(BF16), norm_attn_g=vec3(norm_attn_g), norm_ffn_g=vec3(norm_ffn_g),
        gq=vec3(mla_q_norm_g), gkv=vec3(mla_kv_norm_g), subg=vec3(diff_subln_g),
        lam4=[vec3(a) for a in (diff_lambda_q1, diff_lambda_k1, diff_lambda_q2, diff_lambda_k2)],
        win_sink=win_sink,
        ffn_g=ffn_w_gate.astype(BF16), ffn_u=ffn_w_up.astype(BF16), ffn_d=ffn_w_down.astype(BF16),
        moe_r=jnp.pad(moe_w_router, ((0, 0), (0, 0), (0, LANES - N_EXPERTS))),
        moe_g=moe_w_gate.astype(BF16), moe_u=moe_w_up.astype(BF16), moe_d=moe_w_down.astype(BF16),
        final_g=final_norm_g.reshape(1, D),
    )

    tabs_h = _rope_tables(sl, HD, 0, HD)
    tabs_r = _rope_tables(sl, C_ROPE, KR_OFF, LANES)
    tabs = tabs_h + tabs_r
    ident_block = sl // TM

    y_c, pas, ckvns = _run_group(
        x_prompt.reshape(nbc * sc, D), nbc, sc, lambda i: 0, lambda i: ident_block, tabs, None, mod4, w)

    tiles_per_b = sl // TM
    krp = jnp.pad(cache_mla_krope, ((0, 0), (0, 0), (0, 0), (KR_OFF, LANES - KR_OFF - C_ROPE)))
    caches = (
        (cache_diff_k.reshape(nbl, depth, past, A_HEADS * LANES),
         cache_diff_v.reshape(nbl, depth, past, A_HEADS * LANES)),
        (cache_win_k.reshape(nbl, depth, past, LANES), cache_win_v.reshape(nbl, depth, past, LANES)),
        (cache_mla_ckv, krp),
    )
    y_l, _, _ = _run_group(
        x_sample.reshape(nbl * sl, D), nbl, sl, lambda i: 1 + i // tiles_per_b,
        lambda i: i % tiles_per_b, tabs, caches, mod4, w)

    def cache(per_layer, tail):
        return jnp.stack([a.reshape((nbc, sc) + tail) for a in per_layer], axis=1)

    new_diff_k = cache([p[0] for p in pas], (A_HEADS, 2 * HD))
    new_diff_v = cache([p[1] for p in pas], (A_HEADS, 2 * HD))
    new_win_k = cache([p[2] for p in pas], (B_HEADS // B_GROUP, HD))
    new_win_v = cache([p[3] for p in pas], (B_HEADS // B_GROUP, HD))
    new_mla_ckv = cache(ckvns, (C_KV_LORA,))
    new_mla_krope = cache([p[4] for p in pas], (C_ROPE,))
    return (y_c.reshape(nbc, sc, D), y_l.reshape(nbl, sl, D), new_diff_k, new_diff_v,
            new_win_k, new_win_v, new_mla_ckv, new_mla_krope)
```

```python
import functools
import math

import jax
import jax.numpy as jnp
from jax import lax
from jax.experimental import pallas as pl
from jax.experimental.pallas import tpu as pltpu

F32 = jnp.float32
BF16 = jnp.bfloat16

D = 1024
HD = 64
LANES = 128
GRID_W = 64
BLOCK = 128
WINDOW = 128
A_HEADS = 4
B_HEADS = 8
B_GROUP = 4
C_HEADS = 8
C_Q_LORA = 256
C_KV_LORA = 128
C_NOPE = 64
C_ROPE = 32
C_VDIM = 64
N_EXPERTS = 8
ROPE_THETA = 10000.0
EPS = 1e-6
NEG_INF = -1e30
LOG2E = math.log2(math.e)
MOD_ROWS = 16

TM = 512
TQ = 256
TQB = 512
BM = 512
VMEM_LIMIT = 56 * 1024 * 1024

CH_AQ, CH_AK, CH_AV, CH_BQ, CH_BK, CH_BV, CH_CQ, CH_CKV, CH_CKR = 0, 4, 8, 12, 20, 21, 22, 24, 25
N_CH = 26
NPA = N_CH * LANES
KR_OFF = C_NOPE


def _cparams(sem):
    return pltpu.CompilerParams(dimension_semantics=sem, vmem_limit_bytes=VMEM_LIMIT)


def _rms(x, g):
    return x * lax.rsqrt(jnp.mean(x * x, axis=-1, keepdims=True) + EPS) * g


def _sigmoid(x):
    return 1.0 / (1.0 + jnp.exp(-x))


def _dot(a, b):
    return jnp.dot(a, b, preferred_element_type=F32)


def _dot_t(a, b):
    return lax.dot_general(a, b, (((1,), (1,)), ((), ())), preferred_element_type=F32)


def _ada_kernel(c_ref, w_ref, b_ref, o_ref):
    c = c_ref[...]
    s = (c * _sigmoid(c)).astype(BF16)
    o_ref[...] = _dot(s, w_ref[...].astype(BF16)) + b_ref[...]


def _ada_mod(cond, w_ada, b_ada):
    depth, _, n = w_ada.shape
    tn = 1536
    return pl.pallas_call(
        _ada_kernel,
        out_shape=jax.ShapeDtypeStruct((depth, MOD_ROWS, n), F32),
        grid=(depth, n // tn),
        in_specs=[
            pl.BlockSpec((MOD_ROWS, D), lambda l, j: (0, 0)),
            pl.BlockSpec((None, D, tn), lambda l, j: (l, 0, j)),
            pl.BlockSpec((None, 1, tn), lambda l, j: (l, 0, j)),
        ],
        out_specs=pl.BlockSpec((None, MOD_ROWS, tn), lambda l, j: (l, 0, j)),
        compiler_params=_cparams(("parallel", "parallel")),
        name="ada_mod",
    )(cond, w_ada, b_ada.reshape(depth, 1, n))


def _mod_spec(l, k, row_of_tile):
    return pl.BlockSpec((None, None, 1, D), lambda i: (l, row_of_tile(i), 0, k))


def _norm_mod(x, g_ref, shift_ref, scale_ref):
    return _rms(x, g_ref[...]) * (1.0 + scale_ref[...]) + shift_ref[...]


def _rope(x, c, sa, sb, half):
    return x * c + pltpu.roll(x, LANES - half, 1) * sa + pltpu.roll(x, half, 1) * sb


def _proj_kernel(emit_cache, x_ref, shift_ref, scale_ref, g_ref, w_ref,
                 hc_ref, hsa_ref, hsb_ref, rc_ref, rsa_ref, rsb_ref, gq_ref, gkv_ref,
                 wq_ref, wkn_ref, wv_ref,
                 qa_ref, ka_ref, va_ref, qb_ref, kb_ref, vb_ref, qc_ref, kc_ref, vc_ref, ckvn_ref,
                 *cache_refs):
    h = _norm_mod(x_ref[...], g_ref, shift_ref, scale_ref).astype(BF16)
    pa = _dot(h, w_ref[...])
    hc, hsa, hsb = hc_ref[...], hsa_ref[...], hsb_ref[...]
    rc, rsa, rsb = rc_ref[...], rsa_ref[...], rsb_ref[...]

    def chunk(c, n=1):
        return pa[:, c * LANES:(c + n) * LANES]

    def put(ref, c, val):
        ref[:, c * LANES:(c + 1) * LANES] = val.astype(ref.dtype)

    if emit_cache:
        kv_a_ref, kv_b_ref, kr_ref = cache_refs
        kv_a_ref[...] = chunk(CH_AK, 2 * A_HEADS)
        kv_b_ref[...] = chunk(CH_BK, 2)
        kr_ref[...] = chunk(CH_CKR)

    q_scale = HD ** -0.5
    for c in range(A_HEADS):
        put(qa_ref, c, _rope(chunk(CH_AQ + c), hc, hsa, hsb, HD // 2) * (q_scale * LOG2E))
        put(ka_ref, c, _rope(chunk(CH_AK + c), hc, hsa, hsb, HD // 2))
        put(va_ref, c, chunk(CH_AV + c))
    for c in range(B_HEADS):
        put(qb_ref, c, _rope(chunk(CH_BQ + c), hc, hsa, hsb, HD // 2) * (q_scale * LOG2E))
    put(kb_ref, 0, _rope(chunk(CH_BK), hc, hsa, hsb, HD // 2))
    put(vb_ref, 0, chunk(CH_BV))

    cqn = _rms(chunk(CH_CQ, 2), gq_ref[...]).astype(BF16)
    qup = _dot(cqn, wq_ref[...])
    c_scale = (C_NOPE + C_ROPE) ** -0.5 * LOG2E
    for c in range(C_HEADS):
        put(qc_ref, c, _rope(qup[:, c * LANES:(c + 1) * LANES], rc, rsa, rsb, C_ROPE // 2) * c_scale)

    ckvn = _rms(chunk(CH_CKV), gkv_ref[...])
    ckvn_ref[...] = ckvn
    ckvb = ckvn.astype(BF16)
    kn = _dot(ckvb, wkn_ref[...])
    kr = _rope(chunk(CH_CKR), rc, rsa, rsb, C_ROPE // 2)
    for c in range(C_HEADS):
        put(kc_ref, c, kn[:, c * LANES:(c + 1) * LANES] + kr)
    vc_ref[...] = _with_ones_lane(_dot(ckvb, wv_ref[...])).astype(BF16)


def _proj(x, mod4, norm_g3, w_a, tabs, gq3, gkv3, wq, wkn, wv, l, row_of_tile, tab_block, emit_cache):
    t = x.shape[0]
    tab_spec = pl.BlockSpec((TM, LANES), lambda i: (tab_block(i), 0))
    wide = lambda n: pl.BlockSpec((TM, n), lambda i: (i, 0))
    outs = [(4 * LANES, BF16)] * 3 + [(8 * LANES, BF16), (LANES, BF16), (LANES, BF16)] \
        + [(8 * LANES, BF16)] * 3 + [(LANES, F32)]
    if emit_cache:
        outs += [(2 * A_HEADS * LANES, F32), (2 * LANES, F32), (LANES, F32)]
    return pl.pallas_call(
        functools.partial(_proj_kernel, emit_cache),
        out_shape=[jax.ShapeDtypeStruct((t, n), dt) for n, dt in outs],
        grid=(t // TM,),
        in_specs=[
            wide(D), _mod_spec(l, 0, row_of_tile), _mod_spec(l, 1, row_of_tile),
            pl.BlockSpec((None, 1, D), lambda i: (l, 0, 0)),
            pl.BlockSpec((None, D, NPA), lambda i: (l, 0, 0)),
        ] + [tab_spec] * 6 + [
            pl.BlockSpec((None, 1, C_Q_LORA), lambda i: (l, 0, 0)),
            pl.BlockSpec((None, 1, C_KV_LORA), lambda i: (l, 0, 0)),
            pl.BlockSpec((None, C_Q_LORA, C_HEADS * LANES), lambda i: (l, 0, 0)),
            pl.BlockSpec((None, C_KV_LORA, C_HEADS * LANES), lambda i: (l, 0, 0)),
            pl.BlockSpec((None, C_KV_LORA, C_HEADS * LANES), lambda i: (l, 0, 0)),
        ],
        out_specs=[wide(n) for n, _ in outs],
        compiler_params=_cparams(("parallel",)),
        name="proj",
    )(x, mod4, mod4, norm_g3, w_a, *tabs, gq3, gkv3, wq, wkn, wv)


def _exp2_parts(scores):
    m = functools.reduce(jnp.maximum, [jnp.max(s, axis=-1, keepdims=True) for s in scores])
    es = [jnp.exp2(s - m) for s in scores]
    den = functools.reduce(lambda a, b: a + b, [jnp.sum(e, axis=-1, keepdims=True) for e in es])
    return es, den


def _attn_a_kernel(lam_init, has_ctx, q_ref, k_ref, v_ref, *rest):
    if has_ctx:
        kc_ref, vc_ref, lq1_ref, lk1_ref, lq2_ref, lk2_ref, sg_ref, o_ref = rest
    else:
        lq1_ref, lk1_ref, lq2_ref, lk2_ref, sg_ref, o_ref = rest
    lam = (jnp.exp(jnp.sum(lq1_ref[...] * lk1_ref[...], axis=-1, keepdims=True))
           - jnp.exp(jnp.sum(lq2_ref[...] * lk2_ref[...], axis=-1, keepdims=True)) + lam_init)
    lane = lax.broadcasted_iota(jnp.int32, (q_ref.shape[0], LANES), 1)
    for h in range(A_HEADS):
        sl = slice(h * LANES, (h + 1) * LANES)
        q = q_ref[:, sl]
        zero = jnp.zeros_like(q)
        q1 = jnp.where(lane < HD, q, zero)
        q2 = jnp.where(lane >= HD, q, zero)
        keys, vals = [k_ref[:, sl]], [v_ref[:, sl]]
        if has_ctx:
            keys.append(kc_ref[:, sl].astype(BF16))
            vals.append(vc_ref[:, sl].astype(BF16))
        e1, l1 = _exp2_parts([_dot_t(q1, k) for k in keys])
        e2, l2 = _exp2_parts([_dot_t(q2, k) for k in keys])
        c = lam * l1 / l2
        o = None
        for a1, a2, v in zip(e1, e2, vals):
            part = _dot((a1 - c * a2).astype(BF16), v)
            o = part if o is None else o + part
        o = o * (1.0 / l1)
        o_ref[:, sl] = (_rms(o, sg_ref[...]) * (1.0 - lam_init)).astype(o_ref.dtype)


def _attn_a(qa, ka, va, ctx, lam4, subg3, l, nb, s):
    t = qa.shape[0]
    tq = min(TQ, s)
    nq = s // tq
    w = A_HEADS * LANES
    lam_init = 0.8 - 0.6 * math.exp(-0.3 * l)
    in_specs = [
        pl.BlockSpec((tq, w), lambda b, i: (b * nq + i, 0)),
        pl.BlockSpec((s, w), lambda b, i: (b, 0)),
        pl.BlockSpec((s, w), lambda b, i: (b, 0)),
    ]
    args = [qa, ka, va]
    if ctx is not None:
        past = ctx[0].shape[2]
        in_specs += [pl.BlockSpec((None, None, past, w), lambda b, i: (b, l, 0, 0))] * 2
        args += list(ctx)
    in_specs += [pl.BlockSpec((None, 1, HD), lambda b, i: (l, 0, 0))] * 4
    in_specs += [pl.BlockSpec((None, 1, LANES), lambda b, i: (l, 0, 0))]
    return pl.pallas_call(
        functools.partial(_attn_a_kernel, lam_init, ctx is not None),
        out_shape=jax.ShapeDtypeStruct((t, w), BF16),
        grid=(nb, nq),
        in_specs=in_specs,
        out_specs=pl.BlockSpec((tq, w), lambda b, i: (b * nq + i, 0)),
        compiler_params=_cparams(("parallel", "arbitrary")),
        name="attn_a_lat" if ctx is not None else "attn_a_ctx",
    )(*args, *lam4, subg3)


def _sink_attend(qs, keys, vals, biases, sinkv):
    scores = []
    for k, bias in zip(keys, biases):
        s = _dot_t(qs, k)
        scores.append(s if bias is None else s + bias)
    m = functools.reduce(jnp.maximum, [jnp.max(s, axis=-1, keepdims=True) for s in scores] + [sinkv])
    es = [jnp.exp2(s - m) for s in scores]
    den = functools.reduce(lambda a, b: a + b,
                           [jnp.sum(e, axis=-1, keepdims=True) for e in es] + [jnp.exp2(sinkv - m)])
    o = None
    for e, v in zip(es, vals):
        part = _dot(e.astype(BF16), v)
        o = part if o is None else o + part
    return o * (1.0 / den)


def _stack_group(q_ref, row0, rows, g):
    return jnp.concatenate(
        [q_ref[pl.ds(row0, rows), (B_GROUP * g + j) * LANES:(B_GROUP * g + j + 1) * LANES]
         for j in range(B_GROUP)], axis=0)


def _sink_column(sink_ref, l, g, rows):
    row = lax.broadcasted_iota(jnp.int32, (B_GROUP * rows, 1), 0)
    col = jnp.full((B_GROUP * rows, 1), sink_ref[l, B_GROUP * g + B_GROUP - 1], F32)
    for j in range(B_GROUP - 2, -1, -1):
        col = jnp.where(row < (j + 1) * rows, sink_ref[l, B_GROUP * g + j], col)
    return col * LOG2E


def _store_group(o_ref, row0, rows, g, o):
    lane = lax.broadcasted_iota(jnp.int32, o.shape, 1)
    o = jnp.where((lane >= g * HD) & (lane < (g + 1) * HD), o, 0.0).astype(o_ref.dtype)
    for j in range(B_GROUP):
        c = B_GROUP * g + j
        o_ref[pl.ds(row0, rows), c * LANES:(c + 1) * LANES] = o[j * rows:(j + 1) * rows, :]


def _attn_b_ctx_kernel(l, sink_ref, q_ref, k_ref, v_ref, o_ref):
    rows = q_ref.shape[0]
    k, v = k_ref[...], v_ref[...]
    for g in range(B_HEADS // B_GROUP):
        qs = _stack_group(q_ref, 0, rows, g)
        o = _sink_attend(qs, [k], [v], [None], _sink_column(sink_ref, l, g, rows))
        _store_group(o_ref, 0, rows, g, o)


def _window_bias():
    q = jnp.arange(B_GROUP * BLOCK)[:, None] % BLOCK
    k = jnp.arange(3 * BLOCK)[None, :]
    d = jnp.arange(3)[:, None, None] * BLOCK
    return jnp.where(jnp.abs(d + q - k) <= WINDOW, 0.0, NEG_INF).astype(F32)


def _attn_b_lat_kernel(l, s_len, sink_ref, q_ref, k_ref, v_ref, kc_ref, vc_ref, bias_ref, o_ref):
    qi = pl.program_id(1)
    kc = kc_ref[...].astype(BF16)
    vc = vc_ref[...].astype(BF16)
    win = 3 * BLOCK
    for n in range(TQB // BLOCK):
        blk = qi * (TQB // BLOCK) + n
        first = jnp.clip(blk - 1, 0, s_len // BLOCK - 3)
        ws = pl.multiple_of(first * BLOCK, BLOCK)
        kw = k_ref[pl.ds(ws, win), :]
        vw = v_ref[pl.ds(ws, win), :]
        bias = bias_ref[blk - first]
        for g in range(B_HEADS // B_GROUP):
            qs = _stack_group(q_ref, n * BLOCK, BLOCK, g)
            o = _sink_attend(qs, [kw, kc], [vw, vc], [bias, None], _sink_column(sink_ref, l, g, BLOCK))
            _store_group(o_ref, n * BLOCK, BLOCK, g, o)


def _attn_b(qb, kb, vb, ctx, win_sink, l, nb, s):
    t = qb.shape[0]
    smem = pl.BlockSpec(memory_space=pltpu.SMEM)
    if ctx is None:
        return pl.pallas_call(
            functools.partial(_attn_b_ctx_kernel, l),
            out_shape=jax.ShapeDtypeStruct((t, B_HEADS * LANES), BF16),
            grid=(nb,),
            in_specs=[smem,
                      pl.BlockSpec((s, B_HEADS * LANES), lambda b: (b, 0)),
                      pl.BlockSpec((s, LANES), lambda b: (b, 0)),
                      pl.BlockSpec((s, LANES), lambda b: (b, 0))],
            out_specs=pl.BlockSpec((s, B_HEADS * LANES), lambda b: (b, 0)),
            compiler_params=_cparams(("parallel",)),
            name="attn_b_ctx",
        )(win_sink, qb, kb, vb)
    assert s % TQB == 0 and s >= 3 * BLOCK
    nq = s // TQB
    past = ctx[0].shape[2]
    return pl.pallas_call(
        functools.partial(_attn_b_lat_kernel, l, s),
        out_shape=jax.ShapeDtypeStruct((t, B_HEADS * LANES), BF16),
        grid=(nb, nq),
        in_specs=[smem,
                  pl.BlockSpec((TQB, B_HEADS * LANES), lambda b, i: (b * nq + i, 0)),
                  pl.BlockSpec((s, LANES), lambda b, i: (b, 0)),
                  pl.BlockSpec((s, LANES), lambda b, i: (b, 0)),
                  pl.BlockSpec((None, None, past, LANES), lambda b, i: (b, l, 0, 0)),
                  pl.BlockSpec((None, None, past, LANES), lambda b, i: (b, l, 0, 0)),
                  pl.BlockSpec((3, B_GROUP * BLOCK, 3 * BLOCK), lambda b, i: (0, 0, 0))],
        out_specs=pl.BlockSpec((TQB, B_HEADS * LANES), lambda b, i: (b * nq + i, 0)),
        compiler_params=_cparams(("parallel", "arbitrary")),
        name="attn_b_lat",
    )(win_sink, qb, kb, vb, *ctx, _window_bias())


def _with_ones_lane(v):
    lane = lax.broadcasted_iota(jnp.int32, v.shape, 1)
    return jnp.where(lane % LANES == C_VDIM, 1.0, v)


def _attn_c_kernel(has_ctx, q_ref, k_ref, v_ref, *rest):
    if has_ctx:
        ckv_ref, krp_ref, wkn_ref, wv_ref, o_ref, kc_s, vc_s = rest

        @pl.when(pl.program_id(1) == 0)
        def _():
            ckv = ckv_ref[...].astype(BF16)
            kn = _dot(ckv, wkn_ref[...])
            krp = krp_ref[...]
            for h in range(C_HEADS):
                sl = slice(h * LANES, (h + 1) * LANES)
                kc_s[:, sl] = (kn[:, sl] + krp).astype(BF16)
            vc_s[...] = _with_ones_lane(_dot(ckv, wv_ref[...])).astype(BF16)
    else:
        (o_ref,) = rest
    for h in range(C_HEADS):
        sl = slice(h * LANES, (h + 1) * LANES)
        q = q_ref[:, sl]
        keys, vals = [k_ref[:, sl]], [v_ref[:, sl]]
        if has_ctx:
            keys.append(kc_s[:, sl])
            vals.append(vc_s[:, sl])
        scores = [_dot_t(q, k) for k in keys]
        m = functools.reduce(jnp.maximum, [jnp.max(s, axis=-1, keepdims=True) for s in scores])
        o = None
        for s, v in zip(scores, vals):
            part = _dot(jnp.exp2(s - m).astype(BF16), v)
            o = part if o is None else o + part
        o_ref[:, sl] = (o * (1.0 / o[:, C_VDIM:C_VDIM + 1])).astype(o_ref.dtype)


def _attn_c(qc, kc, vc, ctx, wkn, wv, l, nb, s):
    t = qc.shape[0]
    tq = min(TQ, s)
    nq = s // tq
    w = C_HEADS * LANES
    in_specs = [
        pl.BlockSpec((tq, w), lambda b, i: (b * nq + i, 0)),
        pl.BlockSpec((s, w), lambda b, i: (b, 0)),
        pl.BlockSpec((s, w), lambda b, i: (b, 0)),
    ]
    args = [qc, kc, vc]
    scratch = []
    if ctx is not None:
        past = ctx[0].shape[2]
        in_specs += [pl.BlockSpec((None, None, past, LANES), lambda b, i: (b, l, 0, 0))] * 2
        in_specs += [pl.BlockSpec((None, C_KV_LORA, w), lambda b, i: (l, 0, 0))] * 2
        args += list(ctx) + [wkn, wv]
        scratch = [pltpu.VMEM((past, w), BF16)] * 2
    return pl.pallas_call(
        functools.partial(_attn_c_kernel, ctx is not None),
        out_shape=jax.ShapeDtypeStruct((t, w), BF16),
        grid=(nb, nq),
        in_specs=in_specs,
        out_specs=pl.BlockSpec((tq, w), lambda b, i: (b * nq + i, 0)),
        scratch_shapes=scratch,
        compiler_params=_cparams(("parallel", "arbitrary")),
        name="attn_c_lat" if ctx is not None else "attn_c_ctx",
    )(*args)


def _merge_kernel(x_ref, shift_ref, scale_ref, gate_ref, g_ref, wg_ref, oa_ref, ob_ref, oc_ref,
                  wba_ref, wbb_ref, wbc_ref, wo_ref, o_ref):
    x = x_ref[...]
    h = _norm_mod(x, g_ref, shift_ref, scale_ref).astype(BF16)
    mix = None
    for n, (br_ref, wb_ref) in enumerate(((oa_ref, wba_ref), (ob_ref, wbb_ref), (oc_ref, wbc_ref))):
        gate = _sigmoid(_dot(h, wg_ref[:, n * D:(n + 1) * D]))
        term = gate * _dot(br_ref[...], wb_ref[...])
        mix = term if mix is None else mix + term
    o_ref[...] = x + gate_ref[...] * _dot(mix.astype(BF16), wo_ref[...])


def _merge(x, mod4, norm_g3, wg, oa, ob, oc, wba, wbb, wbc, wo, l, row_of_tile):
    t = x.shape[0]
    row = lambda n: pl.BlockSpec((TM, n), lambda i: (i, 0))
    res = lambda r, c: pl.BlockSpec((None, r, c), lambda i: (l, 0, 0))
    return pl.pallas_call(
        _merge_kernel,
        out_shape=jax.ShapeDtypeStruct((t, D), F32),
        grid=(t // TM,),
        in_specs=[row(D), _mod_spec(l, 0, row_of_tile), _mod_spec(l, 1, row_of_tile),
                  _mod_spec(l, 2, row_of_tile), res(1, D), res(D, 3 * D),
                  row(A_HEADS * LANES), row(B_HEADS * LANES), row(C_HEADS * LANES),
                  res(A_HEADS * LANES, D), res(B_HEADS * LANES, D), res(C_HEADS * LANES, D), res(D, D)],
        out_specs=row(D),
        compiler_params=_cparams(("parallel",)),
        name="merge",
    )(x, mod4, mod4, mod4, norm_g3, wg, oa, ob, oc, wba, wbb, wbc, wo)


def _swiglu(h, wg, wu):
    a = _dot(h, wg)
    return (a * _sigmoid(a) * _dot(h, wu)).astype(BF16)


def _ffn_kernel(tf, x_ref, shift_ref, scale_ref, gate_ref, g_ref, wg_ref, wu_ref, wd_ref, o_ref, acc_ref):
    x = x_ref[...]
    h = _norm_mod(x, g_ref, shift_ref, scale_ref).astype(BF16)
    acc_ref[...] = jnp.zeros_like(acc_ref)

    def body(c, carry):
        off = pl.multiple_of(c * tf, tf)
        a = _swiglu(h, wg_ref[:, pl.ds(off, tf)], wu_ref[:, pl.ds(off, tf)])
        acc_ref[...] += _dot(a, wd_ref[pl.ds(off, tf), :])
        return carry

    lax.fori_loop(0, wg_ref.shape[1] // tf, body, 0)
    o_ref[...] = x + gate_ref[...] * acc_ref[...]


def _ffn(x, mod4, norm_g3, wg, wu, wd, l, idx, row_of_tile):
    t = x.shape[0]
    dff = wg.shape[2]
    tf = 256
    assert dff % tf == 0
    row = pl.BlockSpec((TM, D), lambda i: (i, 0))
    return pl.pallas_call(
        functools.partial(_ffn_kernel, tf),
        out_shape=jax.ShapeDtypeStruct((t, D), F32),
        grid=(t // TM,),
        in_specs=[row, _mod_spec(l, 3, row_of_tile), _mod_spec(l, 4, row_of_tile),
                  _mod_spec(l, 5, row_of_tile),
                  pl.BlockSpec((None, 1, D), lambda i: (l, 0, 0)),
                  pl.BlockSpec((None, D, dff), lambda i: (idx, 0, 0)),
                  pl.BlockSpec((None, D, dff), lambda i: (idx, 0, 0)),
                  pl.BlockSpec((None, dff, D), lambda i: (idx, 0, 0))],
        out_specs=row,
        scratch_shapes=[pltpu.VMEM((TM, D), F32)],
        compiler_params=_cparams(("parallel",)),
        name="ffn",
    )(x, mod4, mod4, mod4, norm_g3, wg, wu, wd)


def _top2(lg, axis, n):
    idx = lax.broadcasted_iota(jnp.int32, lg.shape, axis)
    m1 = jnp.max(lg, axis=axis, keepdims=True)
    i1 = jnp.min(jnp.where(lg == m1, idx, n), axis=axis, keepdims=True)
    rest = jnp.where(idx == i1, NEG_INF, lg)
    m2 = jnp.max(rest, axis=axis, keepdims=True)
    i2 = jnp.min(jnp.where(rest == m2, idx, n), axis=axis, keepdims=True)
    return m1, i1, m2, i2


def _pack_halves(hb):
    lo = pltpu.bitcast(hb[:, :D // 2].astype(F32), jnp.uint32)
    hi = pltpu.bitcast(hb[:, D // 2:].astype(F32), jnp.uint32)
    return (lo >> 16) | hi


def _unpack_halves(u):
    lo = pltpu.bitcast(u << 16, F32).astype(BF16)
    hi = pltpu.bitcast(u & jnp.uint32(0xFFFF0000), F32).astype(BF16)
    return lo, hi


def _route_kernel(x_ref, shift_ref, scale_ref, g_ref, wr_ref, hp_ref, wts_ref, meta_ref, cnt_ref, run_s):
    @pl.when(pl.program_id(0) == 0)
    def _():
        run_s[...] = jnp.zeros_like(run_s)

    h = _norm_mod(x_ref[...], g_ref, shift_ref, scale_ref)
    hp_ref[...] = _pack_halves(h.astype(BF16))
    logits = jnp.dot(h, wr_ref[...], preferred_element_type=F32, precision=lax.Precision.HIGHEST)
    lane = lax.broadcasted_iota(jnp.int32, logits.shape, 1)
    m1, _, m2, _ = _top2(jnp.where(lane < N_EXPERTS, logits, NEG_INF), 1, LANES)
    e2 = jnp.exp(m2 - m1)
    den = 1.0 + e2
    wts_ref[...] = jnp.where(lane == 0, 1.0 / den, jnp.where(lane == 1, e2 / den, 0.0))

    lt = logits.T[:N_EXPERTS, :]
    _, i1, _, i2 = _top2(lt, 0, N_EXPERTS)
    sub = lax.broadcasted_iota(jnp.int32, lt.shape, 0)
    onehot = jnp.where((sub == i1) | (sub == i2), 1.0, 0.0)
    src = lax.broadcasted_iota(jnp.int32, (TM, TM), 0)
    dst = lax.broadcasted_iota(jnp.int32, (TM, TM), 1)
    earlier = jnp.where(src < dst, 1.0, 0.0).astype(BF16)
    before = _dot(onehot.astype(BF16), earlier) + run_s[:, 0:1]
    rank1 = jnp.sum(jnp.where(sub == i1, before, 0.0), axis=0, keepdims=True).astype(jnp.int32)
    rank2 = jnp.sum(jnp.where(sub == i2, before, 0.0), axis=0, keepdims=True).astype(jnp.int32)
    meta_ref[...] = jnp.where(sub == 0, i1, jnp.where(sub == 1, i2, jnp.where(sub == 2, rank1,
                              jnp.where(sub == 3, rank2, 0))))
    run_s[...] = run_s[...] + jnp.sum(onehot, axis=1, keepdims=True)
    cnt_ref[...] = run_s[...]


def _route(x, mod4, norm_g3, wr, l, idx, row_of_tile):
    t = x.shape[0]
    nt = t // TM
    return pl.pallas_call(
        _route_kernel,
        out_shape=[jax.ShapeDtypeStruct((t, D // 2), jnp.uint32), jax.ShapeDtypeStruct((t, LANES), F32),
                   jax.ShapeDtypeStruct((nt, N_EXPERTS, TM), jnp.int32),
                   jax.ShapeDtypeStruct((N_EXPERTS, LANES), F32)],
        grid=(nt,),
        in_specs=[pl.BlockSpec((TM, D), lambda i: (i, 0)),
                  _mod_spec(l, 3, row_of_tile), _mod_spec(l, 4, row_of_tile),
                  pl.BlockSpec((None, 1, D), lambda i: (l, 0, 0)),
                  pl.BlockSpec((None, D, LANES), lambda i: (idx, 0, 0))],
        out_specs=[pl.BlockSpec((TM, D // 2), lambda i: (i, 0)), pl.BlockSpec((TM, LANES), lambda i: (i, 0)),
                   pl.BlockSpec((None, N_EXPERTS, TM), lambda i: (i, 0, 0)),
                   pl.BlockSpec((N_EXPERTS, LANES), lambda i: (0, 0))],
        scratch_shapes=[pltpu.VMEM((N_EXPERTS, LANES), F32)],
        compiler_params=_cparams(("arbitrary",)),
        name="moe_route",
    )(x, mod4, mod4, norm_g3, wr)


def _row_copy(src, src_row, dst, dst_row, sem):
    return pltpu.make_async_copy(src.at[pl.ds(src_row, 1)], dst.at[pl.ds(dst_row, 1)], sem)


ROWS_PER_ISSUE = 8


def _for_each_assignment(fn):
    def trip(g, carry):
        row0 = pl.multiple_of(g * ROWS_PER_ISSUE, ROWS_PER_ISSUE)
        for j in range(ROWS_PER_ISSUE):
            for k in range(2):
                fn(row0 + j, k, 2 * row0 + (2 * j + k))
        return carry

    lax.fori_loop(0, TM // ROWS_PER_ISSUE, trip, 0)


def _scatter_kernel(dest_ref, hp_ref, xs_in_ref, xs_ref, sem):
    del xs_in_ref
    _for_each_assignment(lambda r, k, f: _row_copy(hp_ref, r, xs_ref, dest_ref[0, f], sem).start())
    _for_each_assignment(lambda r, k, f: _row_copy(hp_ref, 0, xs_ref, 0, sem).wait())


def _scatter(dest, hp, xs0):
    nt = dest.shape[0]
    return pl.pallas_call(
        _scatter_kernel,
        out_shape=jax.ShapeDtypeStruct(xs0.shape, xs0.dtype),
        grid=(nt,),
        in_specs=[pl.BlockSpec((None, 1, 2 * TM), lambda i: (i, 0, 0), memory_space=pltpu.SMEM),
                  pl.BlockSpec((TM, D // 2), lambda i: (i, 0)),
                  pl.BlockSpec(memory_space=pl.ANY)],
        out_specs=pl.BlockSpec(memory_space=pl.ANY),
        scratch_shapes=[pltpu.SemaphoreType.DMA(())],
        input_output_aliases={2: 0},
        compiler_params=_cparams(("arbitrary",)),
        name="moe_scatter",
    )(dest, hp, xs0)


def _experts_kernel(blk_in_ref, blk_e_ref, n_used_ref, xs_ref, wg_ref, wu_ref, wd_ref, ys_ref):
    del blk_in_ref, blk_e_ref
    used = pl.program_id(0) < n_used_ref[0]

    @pl.when(used)
    def _():
        lo, hi = _unpack_halves(xs_ref[...])
        half = D // 2
        a = _dot(lo, wg_ref[:half, :]) + _dot(hi, wg_ref[half:, :])
        b = _dot(lo, wu_ref[:half, :]) + _dot(hi, wu_ref[half:, :])
        ys_ref[...] = _dot((a * _sigmoid(a) * b).astype(BF16), wd_ref[...])

    @pl.when(jnp.logical_not(used))
    def _():
        ys_ref[...] = jnp.zeros_like(ys_ref)


def _experts(xs, tables, wg, wu, wd, idx):
    nblk = xs.shape[0] // BM
    dfe = wg.shape[3]
    grid_spec = pltpu.PrefetchScalarGridSpec(
        num_scalar_prefetch=3,
        grid=(nblk,),
        in_specs=[pl.BlockSpec((BM, D // 2), lambda i, bi, be, nu: (bi[i], 0)),
                  pl.BlockSpec((None, None, D, dfe), lambda i, bi, be, nu: (idx, be[i], 0, 0)),
                  pl.BlockSpec((None, None, D, dfe), lambda i, bi, be, nu: (idx, be[i], 0, 0)),
                  pl.BlockSpec((None, None, dfe, D), lambda i, bi, be, nu: (idx, be[i], 0, 0))],
        out_specs=pl.BlockSpec((BM, D), lambda i, bi, be, nu: (i, 0)),
    )
    return pl.pallas_call(
        _experts_kernel,
        out_shape=jax.ShapeDtypeStruct((nblk * BM, D), F32),
        grid_spec=grid_spec,
        compiler_params=_cparams(("arbitrary",)),
        name="moe_experts",
    )(*tables, xs, wg, wu, wd)


def _combine_kernel(dest_ref, x_ref, gate_ref, wts_ref, ys_ref, o_ref, buf, sem):
    _for_each_assignment(lambda r, k, f: _row_copy(ys_ref, dest_ref[0, f], buf.at[k], r, sem).start())
    _for_each_assignment(lambda r, k, f: _row_copy(ys_ref, 0, buf.at[k], 0, sem).wait())
    w = wts_ref[...]
    y = w[:, 0:1] * buf[0] + w[:, 1:2] * buf[1]
    o_ref[...] = x_ref[...] + gate_ref[...] * y


def _combine(dest, x, mod4, wts, ys, l, row_of_tile):
    t = x.shape[0]
    row = pl.BlockSpec((TM, D), lambda i: (i, 0))
    return pl.pallas_call(
        _combine_kernel,
        out_shape=jax.ShapeDtypeStruct((t, D), F32),
        grid=(t // TM,),
        in_specs=[pl.BlockSpec((None, 1, 2 * TM), lambda i: (i, 0, 0), memory_space=pltpu.SMEM),
                  row, _mod_spec(l, 5, row_of_tile),
                  pl.BlockSpec((TM, LANES), lambda i: (i, 0)),
                  pl.BlockSpec(memory_space=pl.ANY)],
        out_specs=row,
        scratch_shapes=[pltpu.VMEM((2, TM, D), F32), pltpu.SemaphoreType.DMA(())],
        compiler_params=_cparams(("arbitrary",)),
        name="moe_combine",
    )(dest, x, mod4, wts, ys)


def _moe(x, mod4, norm_g3, wr, wg, wu, wd, l, idx, row_of_tile):
    t = x.shape[0]
    nblk = 2 * t // BM + N_EXPERTS
    hp, wts, meta, cnt = _route(x, mod4, norm_g3, wr, l, idx, row_of_tile)
    counts = cnt[:, 0].astype(jnp.int32)
    blocks = (counts + BM - 1) // BM
    ends = jnp.cumsum(blocks)
    n_used = ends[-1]
    base = (ends - blocks) * BM
    dest = jnp.stack([base[meta[:, 0]] + meta[:, 2], base[meta[:, 1]] + meta[:, 3]], axis=-1)
    dest = dest.reshape(dest.shape[0], 1, 2 * TM)
    step = jnp.arange(nblk, dtype=jnp.int32)
    blk_in = jnp.minimum(step, jnp.maximum(n_used - 1, 0))
    blk_e = jnp.minimum(jnp.searchsorted(ends, blk_in, side="right"), N_EXPERTS - 1).astype(jnp.int32)
    tables = (blk_in, blk_e, n_used.reshape(1))
    xs = _scatter(dest, hp, jnp.zeros((nblk * BM, D // 2), jnp.uint32))
    ys = _experts(xs, tables, wg, wu, wd, idx)
    return _combine(dest, x, mod4, wts, ys, l, row_of_tile)


def _final_kernel(x_ref, g_ref, o_ref):
    o_ref[...] = _rms(x_ref[...], g_ref[...])


def _final_norm(x, g2):
    t = x.shape[0]
    return pl.pallas_call(
        _final_kernel,
        out_shape=jax.ShapeDtypeStruct((t, D), F32),
        grid=(t // TM,),
        in_specs=[pl.BlockSpec((TM, D), lambda i: (i, 0)), pl.BlockSpec((1, D), lambda i: (0, 0))],
        out_specs=pl.BlockSpec((TM, D), lambda i: (i, 0)),
        compiler_params=_cparams(("parallel",)),
        name="final_norm",
    )(x, g2)


def _relayout_w_in(w_in):
    depth = w_in.shape[0]
    z = lambda n: jnp.zeros((depth, D, n), w_in.dtype)
    aq, ak, av = w_in[..., 0:512], w_in[..., 512:1024], w_in[..., 1024:1536]
    bq, bk, bv = w_in[..., 1536:2048], w_in[..., 2048:2176], w_in[..., 2176:2304]
    cq, ckv, ckr = w_in[..., 2304:2560], w_in[..., 2560:2688], w_in[..., 2688:2720]
    gates = w_in[..., 2720:2720 + 3 * D]
    pieces = [aq, ak, av]
    for h in range(B_HEADS):
        qh = bq[..., h * HD:(h + 1) * HD]
        pieces += [qh, z(HD)] if h // B_GROUP == 0 else [z(HD), qh]
    pieces += [bk, bv, cq, ckv, z(KR_OFF), ckr, z(LANES - KR_OFF - C_ROPE)]
    w_a = jnp.concatenate(pieces, axis=-1).astype(BF16)
    assert w_a.shape[-1] == NPA
    return w_a, gates.astype(BF16)


def _relayout_mla(w_q_up, w_kv_up):
    depth = w_q_up.shape[0]
    qd = C_NOPE + C_ROPE
    wq = jnp.pad(w_q_up.reshape(depth, C_Q_LORA, C_HEADS, qd), ((0, 0), (0, 0), (0, 0), (0, LANES - qd)))
    kv = w_kv_up.reshape(depth, C_KV_LORA, C_HEADS, C_NOPE + C_VDIM)
    wkn = jnp.pad(kv[..., :C_NOPE], ((0, 0), (0, 0), (0, 0), (0, LANES - C_NOPE)))
    wv = jnp.pad(kv[..., C_NOPE:], ((0, 0), (0, 0), (0, 0), (0, LANES - C_VDIM)))
    flat = lambda w: w.reshape(depth, w.shape[1], C_HEADS * LANES).astype(BF16)
    return flat(wq), flat(wkn), flat(wv)


def _relayout_w_branch(w_branch):
    depth = w_branch.shape[0]
    wba = w_branch[:, 0]
    b = w_branch[:, 1].reshape(depth, B_HEADS, HD, D)
    zb = jnp.zeros_like(b)
    lo = jnp.concatenate([b, zb], axis=2)
    hi = jnp.concatenate([zb, b], axis=2)
    grp = (jnp.arange(B_HEADS) // B_GROUP).reshape(1, B_HEADS, 1, 1)
    wbb = jnp.where(grp == 0, lo, hi).reshape(depth, B_HEADS * LANES, D)
    c = w_branch[:, 2].reshape(depth, C_HEADS, C_VDIM, D)
    wbc = jnp.pad(c, ((0, 0), (0, 0), (0, LANES - C_VDIM), (0, 0))).reshape(depth, C_HEADS * LANES, D)
    return wba.astype(BF16), wbb.astype(BF16), wbc.astype(BF16)


def _rope_tables(n_tokens, rot_dim, lane_off, group):
    rows_n = n_tokens // GRID_W
    row, col = jnp.meshgrid(jnp.arange(rows_n), jnp.arange(GRID_W), indexing="ij")
    row = row.reshape(-1).astype(F32)
    col = col.reshape(-1).astype(F32)
    n_freq = rot_dim // 4
    inv = jnp.power(ROPE_THETA, -jnp.arange(n_freq, dtype=F32) / n_freq)
    ang = jnp.concatenate([row[:, None] * inv, col[:, None] * inv], axis=-1)
    cos, sin = jnp.cos(ang), jnp.sin(ang)
    half = rot_dim // 2
    c = jnp.ones((n_tokens, LANES), F32)
    sa = jnp.zeros((n_tokens, LANES), F32)
    sb = jnp.zeros((n_tokens, LANES), F32)
    for off in range(lane_off, LANES - rot_dim + 1, group):
        c = c.at[:, off:off + half].set(cos).at[:, off + half:off + rot_dim].set(cos)
        sa = sa.at[:, off:off + half].set(-sin)
        sb = sb.at[:, off + half:off + rot_dim].set(sin)
        if group >= LANES:
            break
    ident = (jnp.ones((TM, LANES), F32), jnp.zeros((TM, LANES), F32), jnp.zeros((TM, LANES), F32))
    return tuple(jnp.concatenate([a, i], axis=0) for a, i in zip((c, sa, sb), ident))


def _run_group(x, nb, s, row_of_tile, tab_block, tabs, caches, mod4, w):
    depth = w["w_a"].shape[0]
    t = nb * s
    assert t % TM == 0 and (s % TM == 0 or TM % s == 0)
    pas, ckvns = [], []
    for l in range(depth):
        qa, ka, va, qb, kb, vb, qc, kc, vc, ckvn, *raw = _proj(
            x, mod4, w["norm_attn_g"], w["w_a"], tabs, w["gq"], w["gkv"], w["wq"], w["wkn"], w["wv"],
            l, row_of_tile, tab_block, caches is None)
        if caches is None:
            ctx_a = ctx_b = ctx_c = None
            kv_a, kv_b, kr = raw
            aw = A_HEADS * LANES
            pas.append((kv_a[:, :aw], kv_a[:, aw:], kv_b[:, :LANES], kv_b[:, LANES:],
                        kr[:, KR_OFF:KR_OFF + C_ROPE]))
            ckvns.append(ckvn)
        else:
            ctx_a, ctx_b, ctx_c = caches
        oa = _attn_a(qa, ka, va, ctx_a, w["lam4"], w["subg"], l, nb, s)
        ob = _attn_b(qb, kb, vb, ctx_b, w["win_sink"], l, nb, s)
        oc = _attn_c(qc, kc, vc, ctx_c, w["wkn"], w["wv"], l, nb, s)
        x = _merge(x, mod4, w["norm_attn_g"], w["w_gates"], oa, ob, oc,
                   w["wba"], w["wbb"], w["wbc"], w["w_out"], l, row_of_tile)
        if l % 2 == 0:
            x = _ffn(x, mod4, w["norm_ffn_g"], w["ffn_g"], w["ffn_u"], w["ffn_d"], l, l // 2, row_of_tile)
        else:
            x = _moe(x, mod4, w["norm_ffn_g"], w["moe_r"], w["moe_g"], w["moe_u"], w["moe_d"],
                     l, l // 2, row_of_tile)
    return _final_norm(x, w["final_g"]), pas, ckvns


def kernel(x_prompt, x_sample, cache_diff_k, cache_diff_v, cache_win_k, cache_win_v, cache_mla_ckv, cache_mla_krope, c, c_ctx, w_ada, b_ada, norm_attn_g, norm_ffn_g, w_in, diff_lambda_q1, diff_lambda_k1, diff_lambda_q2, diff_lambda_k2, diff_subln_g, win_sink, mla_q_norm_g, mla_w_q_up, mla_kv_norm_g, mla_w_kv_up, w_branch, w_out, ffn_w_gate, ffn_w_up, ffn_w_down, moe_w_router, moe_w_gate, moe_w_up, moe_w_down, final_norm_g):
    depth = w_in.shape[0]
    nbc, sc, _ = x_prompt.shape
    nbl, sl, _ = x_sample.shape
    past = cache_diff_k.shape[2]
    assert 1 + nbl <= MOD_ROWS

    cond = jnp.zeros((MOD_ROWS, D), F32).at[0].set(c_ctx).at[1:1 + nbl].set(c)
    mod4 = _ada_mod(cond, w_ada, b_ada).reshape(depth, MOD_ROWS, 1, 6 * D)

    w_a, w_gates = _relayout_w_in(w_in)
    wq, wkn, wv = _relayout_mla(mla_w_q_up, mla_w_kv_up)
    wba, wbb, wbc = _relayout_w_branch(w_branch)
    vec3 = lambda a: a.reshape(a.shape[0], 1, a.shape[1])
    w = dict(
        w_a=w_a, w_gates=w_gates, wq=wq, wkn=wkn, wv=wv, wba=wba, wbb=wbb, wbc=wbc,
        w_out=w_out.astype(BF16), norm_attn_g=vec3(norm_attn_g), norm_ffn_g=vec3(norm_ffn_g),
        gq=vec3(mla_q_norm_g), gkv=vec3(mla_kv_norm_g), subg=vec3(diff_subln_g),
        lam4=[vec3(a) for a in (diff_lambda_q1, diff_lambda_k1, diff_lambda_q2, diff_lambda_k2)],
        win_sink=win_sink,
        ffn_g=ffn_w_gate.astype(BF16), ffn_u=ffn_w_up.astype(BF16), ffn_d=ffn_w_down.astype(BF16),
        moe_r=jnp.pad(moe_w_router, ((0, 0), (0, 0), (0, LANES - N_EXPERTS))),
        moe_g=moe_w_gate.astype(BF16), moe_u=moe_w_up.astype(BF16), moe_d=moe_w_down.astype(BF16),
        final_g=final_norm_g.reshape(1, D),
    )

    tabs_h = _rope_tables(sl, HD, 0, HD)
    tabs_r = _rope_tables(sl, C_ROPE, KR_OFF, LANES)
    tabs = tabs_h + tabs_r
    ident_block = sl // TM

    y_c, pas, ckvns = _run_group(
        x_prompt.reshape(nbc * sc, D), nbc, sc, lambda i: 0, lambda i: ident_block, tabs, None, mod4, w)

    tiles_per_b = sl // TM
    krp = jnp.pad(cache_mla_krope, ((0, 0), (0, 0), (0, 0), (KR_OFF, LANES - KR_OFF - C_ROPE)))
    caches = (
        (cache_diff_k.reshape(nbl, depth, past, A_HEADS * LANES),
         cache_diff_v.reshape(nbl, depth, past, A_HEADS * LANES)),
        (cache_win_k.reshape(nbl, depth, past, LANES), cache_win_v.reshape(nbl, depth, past, LANES)),
        (cache_mla_ckv, krp),
    )
    y_l, _, _ = _run_group(
        x_sample.reshape(nbl * sl, D), nbl, sl, lambda i: 1 + i // tiles_per_b,
        lambda i: i % tiles_per_b, tabs, caches, mod4, w)

    def cache(per_layer, tail):
        return jnp.stack([a.reshape((nbc, sc) + tail) for a in per_layer], axis=1)

    new_diff_k = cache([p[0] for p in pas], (A_HEADS, 2 * HD))
    new_diff_v = cache([p[1] for p in pas], (A_HEADS, 2 * HD))
    new_win_k = cache([p[2] for p in pas], (B_HEADS // B_GROUP, HD))
    new_win_v = cache([p[3] for p in pas], (B_HEADS // B_GROUP, HD))
    new_mla_ckv = cache(ckvns, (C_KV_LORA,))
    new_mla_krope = cache([p[4] for p in pas], (C_ROPE,))
    return (y_c.reshape(nbc, sc, D), y_l.reshape(nbl, sl, D), new_diff_k, new_diff_v,
            new_win_k, new_win_v, new_mla_ckv, new_mla_krope)
```

```python
import functools
import math

import jax
import jax.numpy as jnp
from jax import lax
from jax.experimental import pallas as pl
from jax.experimental.pallas import tpu as pltpu

F32 = jnp.float32
BF16 = jnp.bfloat16

D = 1024
HD = 64
LANES = 128
GRID_W = 64
BLOCK = 128
WINDOW = 128
A_HEADS = 4
B_HEADS = 8
B_GROUP = 4
C_HEADS = 8
C_Q_LORA = 256
C_KV_LORA = 128
C_NOPE = 64
C_ROPE = 32
C_VDIM = 64
N_EXPERTS = 8
ROPE_THETA = 10000.0
EPS = 1e-6
NEG_INF = -1e30
LOG2E = math.log2(math.e)
MOD_ROWS = 16

TM = 512
TQ = 256
TQC = 512
TQB = 512
BM = 512
VMEM_LIMIT = 56 * 1024 * 1024

CH_AQ, CH_AK, CH_AV, CH_BQ, CH_BK, CH_BV, CH_CQ, CH_CKV, CH_CKR = 0, 4, 8, 12, 20, 21, 22, 24, 25
N_CH = 26
NPA = N_CH * LANES
KR_OFF = C_NOPE


def _cparams(sem):
    return pltpu.CompilerParams(dimension_semantics=sem, vmem_limit_bytes=VMEM_LIMIT)


def _rms(x, g):
    return x * lax.rsqrt(jnp.mean(x * x, axis=-1, keepdims=True) + EPS) * g


def _sigmoid(x):
    return 1.0 / (1.0 + jnp.exp(-x))


def _dot(a, b):
    return jnp.dot(a, b, preferred_element_type=F32)


def _dot_t(a, b):
    return lax.dot_general(a, b, (((1,), (1,)), ((), ())), preferred_element_type=F32)


def _ada_kernel(c_ref, w_ref, b_ref, o_ref):
    c = c_ref[...]
    s = (c * _sigmoid(c)).astype(BF16)
    o_ref[...] = _dot(s, w_ref[...].astype(BF16)) + b_ref[...]


def _ada_mod(cond, w_ada, b_ada):
    depth, _, n = w_ada.shape
    tn = 1536
    return pl.pallas_call(
        _ada_kernel,
        out_shape=jax.ShapeDtypeStruct((depth, MOD_ROWS, n), F32),
        grid=(depth, n // tn),
        in_specs=[
            pl.BlockSpec((MOD_ROWS, D), lambda l, j: (0, 0)),
            pl.BlockSpec((None, D, tn), lambda l, j: (l, 0, j)),
            pl.BlockSpec((None, 1, tn), lambda l, j: (l, 0, j)),
        ],
        out_specs=pl.BlockSpec((None, MOD_ROWS, tn), lambda l, j: (l, 0, j)),
        compiler_params=_cparams(("parallel", "parallel")),
        name="ada_mod",
    )(cond, w_ada, b_ada.reshape(depth, 1, n))


def _mod_spec(l, k, row_of_tile):
    return pl.BlockSpec((None, None, 1, D), lambda i: (l, row_of_tile(i), 0, k))


def _norm_mod(x, g_ref, shift_ref, scale_ref):
    return _rms(x, g_ref[...]) * (1.0 + scale_ref[...]) + shift_ref[...]


def _rope(x, c, sa, sb, half):
    return x * c + pltpu.roll(x, LANES - half, 1) * sa + pltpu.roll(x, half, 1) * sb


def _proj_kernel(emit_cache, x_ref, shift_ref, scale_ref, g_ref, w_ref,
                 hc_ref, hsa_ref, hsb_ref, rc_ref, rsa_ref, rsb_ref, gq_ref, gkv_ref,
                 wq_ref, wkn_ref, wv_ref,
                 qa_ref, ka_ref, va_ref, qb_ref, kb_ref, vb_ref, qc_ref, kc_ref, vc_ref, ckvn_ref,
                 *cache_refs):
    h = _norm_mod(x_ref[...], g_ref, shift_ref, scale_ref).astype(BF16)
    pa = _dot(h, w_ref[...])
    hc, hsa, hsb = hc_ref[...], hsa_ref[...], hsb_ref[...]
    rc, rsa, rsb = rc_ref[...], rsa_ref[...], rsb_ref[...]

    def chunk(c, n=1):
        return pa[:, c * LANES:(c + n) * LANES]

    def put(ref, c, val):
        ref[:, c * LANES:(c + 1) * LANES] = val.astype(ref.dtype)

    if emit_cache:
        kv_a_ref, kv_b_ref, kr_ref = cache_refs
        kv_a_ref[...] = chunk(CH_AK, 2 * A_HEADS)
        kv_b_ref[...] = chunk(CH_BK, 2)
        kr_ref[...] = chunk(CH_CKR)

    q_scale = HD ** -0.5
    for c in range(A_HEADS):
        put(qa_ref, c, _rope(chunk(CH_AQ + c), hc, hsa, hsb, HD // 2) * (q_scale * LOG2E))
        put(ka_ref, c, _rope(chunk(CH_AK + c), hc, hsa, hsb, HD // 2))
        put(va_ref, c, chunk(CH_AV + c))
    for c in range(B_HEADS):
        put(qb_ref, c, _rope(chunk(CH_BQ + c), hc, hsa, hsb, HD // 2) * (q_scale * LOG2E))
    put(kb_ref, 0, _rope(chunk(CH_BK), hc, hsa, hsb, HD // 2))
    put(vb_ref, 0, chunk(CH_BV))

    cqn = _rms(chunk(CH_CQ, 2), gq_ref[...]).astype(BF16)
    qup = _dot(cqn, wq_ref[...])
    c_scale = (C_NOPE + C_ROPE) ** -0.5 * LOG2E
    for c in range(C_HEADS):
        put(qc_ref, c, _rope(qup[:, c * LANES:(c + 1) * LANES], rc, rsa, rsb, C_ROPE // 2) * c_scale)

    ckvn = _rms(chunk(CH_CKV), gkv_ref[...])
    ckvn_ref[...] = ckvn
    ckvb = ckvn.astype(BF16)
    kn = _dot(ckvb, wkn_ref[...])
    kr = _rope(chunk(CH_CKR), rc, rsa, rsb, C_ROPE // 2)
    for c in range(C_HEADS):
        put(kc_ref, c, kn[:, c * LANES:(c + 1) * LANES] + kr)
    vc_ref[...] = _with_ones_lane(_dot(ckvb, wv_ref[...])).astype(BF16)


def _proj(x, mod4, norm_g3, w_a, tabs, gq3, gkv3, wq, wkn, wv, l, row_of_tile, tab_block, emit_cache):
    t = x.shape[0]
    tab_spec = pl.BlockSpec((TM, LANES), lambda i: (tab_block(i), 0))
    wide = lambda n: pl.BlockSpec((TM, n), lambda i: (i, 0))
    outs = [(4 * LANES, BF16)] * 3 + [(8 * LANES, BF16), (LANES, BF16), (LANES, BF16)] \
        + [(8 * LANES, BF16)] * 3 + [(LANES, F32)]
    if emit_cache:
        outs += [(2 * A_HEADS * LANES, F32), (2 * LANES, F32), (LANES, F32)]
    return pl.pallas_call(
        functools.partial(_proj_kernel, emit_cache),
        out_shape=[jax.ShapeDtypeStruct((t, n), dt) for n, dt in outs],
        grid=(t // TM,),
        in_specs=[
            wide(D), _mod_spec(l, 0, row_of_tile), _mod_spec(l, 1, row_of_tile),
            pl.BlockSpec((None, 1, D), lambda i: (l, 0, 0)),
            pl.BlockSpec((None, D, NPA), lambda i: (l, 0, 0)),
        ] + [tab_spec] * 6 + [
            pl.BlockSpec((None, 1, C_Q_LORA), lambda i: (l, 0, 0)),
            pl.BlockSpec((None, 1, C_KV_LORA), lambda i: (l, 0, 0)),
            pl.BlockSpec((None, C_Q_LORA, C_HEADS * LANES), lambda i: (l, 0, 0)),
            pl.BlockSpec((None, C_KV_LORA, C_HEADS * LANES), lambda i: (l, 0, 0)),
            pl.BlockSpec((None, C_KV_LORA, C_HEADS * LANES), lambda i: (l, 0, 0)),
        ],
        out_specs=[wide(n) for n, _ in outs],
        compiler_params=_cparams(("parallel",)),
        name="proj",
    )(x, mod4, mod4, norm_g3, w_a, *tabs, gq3, gkv3, wq, wkn, wv)


def _exp2_parts(scores):
    m = functools.reduce(jnp.maximum, [jnp.max(s, axis=-1, keepdims=True) for s in scores])
    es = [jnp.exp2(s - m) for s in scores]
    den = functools.reduce(lambda a, b: a + b, [jnp.sum(e, axis=-1, keepdims=True) for e in es])
    return es, den


def _attn_a_kernel(lam_init, has_ctx, q_ref, k_ref, v_ref, *rest):
    if has_ctx:
        kc_ref, vc_ref, lq1_ref, lk1_ref, lq2_ref, lk2_ref, sg_ref, o_ref = rest
    else:
        lq1_ref, lk1_ref, lq2_ref, lk2_ref, sg_ref, o_ref = rest
    lam = (jnp.exp(jnp.sum(lq1_ref[...] * lk1_ref[...], axis=-1, keepdims=True))
           - jnp.exp(jnp.sum(lq2_ref[...] * lk2_ref[...], axis=-1, keepdims=True)) + lam_init)
    lane = lax.broadcasted_iota(jnp.int32, (q_ref.shape[0], LANES), 1)
    for h in range(A_HEADS):
        sl = slice(h * LANES, (h + 1) * LANES)
        q = q_ref[:, sl]
        zero = jnp.zeros_like(q)
        q1 = jnp.where(lane < HD, q, zero)
        q2 = jnp.where(lane >= HD, q, zero)
        keys, vals = [k_ref[:, sl]], [v_ref[:, sl]]
        if has_ctx:
            keys.append(kc_ref[:, sl].astype(BF16))
            vals.append(vc_ref[:, sl].astype(BF16))
        e1, l1 = _exp2_parts([_dot_t(q1, k) for k in keys])
        e2, l2 = _exp2_parts([_dot_t(q2, k) for k in keys])
        c = lam * l1 / l2
        o = None
        for a1, a2, v in zip(e1, e2, vals):
            part = _dot((a1 - c * a2).astype(BF16), v)
            o = part if o is None else o + part
        o = o * (1.0 / l1)
        o_ref[:, sl] = (_rms(o, sg_ref[...]) * (1.0 - lam_init)).astype(o_ref.dtype)


def _attn_a(qa, ka, va, ctx, lam4, subg3, l, nb, s):
    t = qa.shape[0]
    tq = min(TQ, s)
    nq = s // tq
    w = A_HEADS * LANES
    lam_init = 0.8 - 0.6 * math.exp(-0.3 * l)
    in_specs = [
        pl.BlockSpec((tq, w), lambda b, i: (b * nq + i, 0)),
        pl.BlockSpec((s, w), lambda b, i: (b, 0)),
        pl.BlockSpec((s, w), lambda b, i: (b, 0)),
    ]
    args = [qa, ka, va]
    if ctx is not None:
        past = ctx[0].shape[2]
        in_specs += [pl.BlockSpec((None, None, past, w), lambda b, i: (b, l, 0, 0))] * 2
        args += list(ctx)
    in_specs += [pl.BlockSpec((None, 1, HD), lambda b, i: (l, 0, 0))] * 4
    in_specs += [pl.BlockSpec((None, 1, LANES), lambda b, i: (l, 0, 0))]
    return pl.pallas_call(
        functools.partial(_attn_a_kernel, lam_init, ctx is not None),
        out_shape=jax.ShapeDtypeStruct((t, w), BF16),
        grid=(nb, nq),
        in_specs=in_specs,
        out_specs=pl.BlockSpec((tq, w), lambda b, i: (b * nq + i, 0)),
        compiler_params=_cparams(("parallel", "arbitrary")),
        name="attn_a_lat" if ctx is not None else "attn_a_ctx",
    )(*args, *lam4, subg3)


def _sink_attend(qs, keys, vals, biases, sinkv):
    scores = []
    for k, bias in zip(keys, biases):
        s = _dot_t(qs, k)
        scores.append(s if bias is None else s + bias)
    m = functools.reduce(jnp.maximum, [jnp.max(s, axis=-1, keepdims=True) for s in scores] + [sinkv])
    es = [jnp.exp2(s - m) for s in scores]
    den = functools.reduce(lambda a, b: a + b,
                           [jnp.sum(e, axis=-1, keepdims=True) for e in es] + [jnp.exp2(sinkv - m)])
    o = None
    for e, v in zip(es, vals):
        part = _dot(e.astype(BF16), v)
        o = part if o is None else o + part
    return o * (1.0 / den)


def _stack_group(q_ref, row0, rows, g):
    return jnp.concatenate(
        [q_ref[pl.ds(row0, rows), (B_GROUP * g + j) * LANES:(B_GROUP * g + j + 1) * LANES]
         for j in range(B_GROUP)], axis=0)


def _sink_column(sink_ref, l, g, rows):
    row = lax.broadcasted_iota(jnp.int32, (B_GROUP * rows, 1), 0)
    col = jnp.full((B_GROUP * rows, 1), sink_ref[l, B_GROUP * g + B_GROUP - 1], F32)
    for j in range(B_GROUP - 2, -1, -1):
        col = jnp.where(row < (j + 1) * rows, sink_ref[l, B_GROUP * g + j], col)
    return col * LOG2E


def _store_group(o_ref, row0, rows, g, o):
    lane = lax.broadcasted_iota(jnp.int32, o.shape, 1)
    o = jnp.where((lane >= g * HD) & (lane < (g + 1) * HD), o, 0.0).astype(o_ref.dtype)
    for j in range(B_GROUP):
        c = B_GROUP * g + j
        o_ref[pl.ds(row0, rows), c * LANES:(c + 1) * LANES] = o[j * rows:(j + 1) * rows, :]


def _attn_b_ctx_kernel(l, sink_ref, q_ref, k_ref, v_ref, o_ref):
    rows = q_ref.shape[0]
    k, v = k_ref[...], v_ref[...]
    for g in range(B_HEADS // B_GROUP):
        qs = _stack_group(q_ref, 0, rows, g)
        o = _sink_attend(qs, [k], [v], [None], _sink_column(sink_ref, l, g, rows))
        _store_group(o_ref, 0, rows, g, o)


def _window_bias():
    q = jnp.arange(B_GROUP * BLOCK)[:, None] % BLOCK
    k = jnp.arange(3 * BLOCK)[None, :]
    d = jnp.arange(3)[:, None, None] * BLOCK
    return jnp.where(jnp.abs(d + q - k) <= WINDOW, 0.0, NEG_INF).astype(F32)


def _attn_b_lat_kernel(l, s_len, sink_ref, q_ref, k_ref, v_ref, kc_ref, vc_ref, bias_ref, o_ref):
    qi = pl.program_id(1)
    kc = kc_ref[...].astype(BF16)
    vc = vc_ref[...].astype(BF16)
    win = 3 * BLOCK
    for n in range(TQB // BLOCK):
        blk = qi * (TQB // BLOCK) + n
        first = jnp.clip(blk - 1, 0, s_len // BLOCK - 3)
        ws = pl.multiple_of(first * BLOCK, BLOCK)
        kw = k_ref[pl.ds(ws, win), :]
        vw = v_ref[pl.ds(ws, win), :]
        bias = bias_ref[blk - first]
        for g in range(B_HEADS // B_GROUP):
            qs = _stack_group(q_ref, n * BLOCK, BLOCK, g)
            o = _sink_attend(qs, [kw, kc], [vw, vc], [bias, None], _sink_column(sink_ref, l, g, BLOCK))
            _store_group(o_ref, n * BLOCK, BLOCK, g, o)


def _attn_b(qb, kb, vb, ctx, win_sink, l, nb, s):
    t = qb.shape[0]
    smem = pl.BlockSpec(memory_space=pltpu.SMEM)
    if ctx is None:
        return pl.pallas_call(
            functools.partial(_attn_b_ctx_kernel, l),
            out_shape=jax.ShapeDtypeStruct((t, B_HEADS * LANES), BF16),
            grid=(nb,),
            in_specs=[smem,
                      pl.BlockSpec((s, B_HEADS * LANES), lambda b: (b, 0)),
                      pl.BlockSpec((s, LANES), lambda b: (b, 0)),
                      pl.BlockSpec((s, LANES), lambda b: (b, 0))],
            out_specs=pl.BlockSpec((s, B_HEADS * LANES), lambda b: (b, 0)),
            compiler_params=_cparams(("parallel",)),
            name="attn_b_ctx",
        )(win_sink, qb, kb, vb)
    assert s % TQB == 0 and s >= 3 * BLOCK
    nq = s // TQB
    past = ctx[0].shape[2]
    return pl.pallas_call(
        functools.partial(_attn_b_lat_kernel, l, s),
        out_shape=jax.ShapeDtypeStruct((t, B_HEADS * LANES), BF16),
        grid=(nb, nq),
        in_specs=[smem,
                  pl.BlockSpec((TQB, B_HEADS * LANES), lambda b, i: (b * nq + i, 0)),
                  pl.BlockSpec((s, LANES), lambda b, i: (b, 0)),
                  pl.BlockSpec((s, LANES), lambda b, i: (b, 0)),
                  pl.BlockSpec((None, None, past, LANES), lambda b, i: (b, l, 0, 0)),
                  pl.BlockSpec((None, None, past, LANES), lambda b, i: (b, l, 0, 0)),
                  pl.BlockSpec((3, B_GROUP * BLOCK, 3 * BLOCK), lambda b, i: (0, 0, 0))],
        out_specs=pl.BlockSpec((TQB, B_HEADS * LANES), lambda b, i: (b * nq + i, 0)),
        compiler_params=_cparams(("parallel", "arbitrary")),
        name="attn_b_lat",
    )(win_sink, qb, kb, vb, *ctx, _window_bias())


def _with_ones_lane(v):
    lane = lax.broadcasted_iota(jnp.int32, v.shape, 1)
    return jnp.where(lane % LANES == C_VDIM, 1.0, v)


def _attn_c_kernel(has_ctx, q_ref, k_ref, v_ref, *rest):
    if has_ctx:
        ckv_ref, krp_ref, wkn_ref, wv_ref, o_ref, kc_s, vc_s = rest

        @pl.when(pl.program_id(1) == 0)
        def _():
            ckv = ckv_ref[...].astype(BF16)
            kn = _dot(ckv, wkn_ref[...])
            krp = krp_ref[...]
            for h in range(C_HEADS):
                sl = slice(h * LANES, (h + 1) * LANES)
                kc_s[:, sl] = (kn[:, sl] + krp).astype(BF16)
            vc_s[...] = _with_ones_lane(_dot(ckv, wv_ref[...])).astype(BF16)
    else:
        (o_ref,) = rest
    for h in range(C_HEADS):
        sl = slice(h * LANES, (h + 1) * LANES)
        q = q_ref[:, sl]
        keys, vals = [k_ref[:, sl]], [v_ref[:, sl]]
        if has_ctx:
            keys.append(kc_s[:, sl])
            vals.append(vc_s[:, sl])
        scores = [_dot_t(q, k) for k in keys]
        m = functools.reduce(jnp.maximum, [jnp.max(s, axis=-1, keepdims=True) for s in scores])
        o = None
        for s, v in zip(scores, vals):
            part = _dot(jnp.exp2(s - m).astype(BF16), v)
            o = part if o is None else o + part
        o_ref[:, sl] = (o * (1.0 / o[:, C_VDIM:C_VDIM + 1])).astype(o_ref.dtype)


def _attn_c(qc, kc, vc, ctx, wkn, wv, l, nb, s):
    t = qc.shape[0]
    tq = min(TQC, s)
    nq = s // tq
    w = C_HEADS * LANES
    in_specs = [
        pl.BlockSpec((tq, w), lambda b, i: (b * nq + i, 0)),
        pl.BlockSpec((s, w), lambda b, i: (b, 0)),
        pl.BlockSpec((s, w), lambda b, i: (b, 0)),
    ]
    args = [qc, kc, vc]
    scratch = []
    if ctx is not None:
        past = ctx[0].shape[2]
        in_specs += [pl.BlockSpec((None, None, past, LANES), lambda b, i: (b, l, 0, 0))] * 2
        in_specs += [pl.BlockSpec((None, C_KV_LORA, w), lambda b, i: (l, 0, 0))] * 2
        args += list(ctx) + [wkn, wv]
        scratch = [pltpu.VMEM((past, w), BF16)] * 2
    return pl.pallas_call(
        functools.partial(_attn_c_kernel, ctx is not None),
        out_shape=jax.ShapeDtypeStruct((t, w), BF16),
        grid=(nb, nq),
        in_specs=in_specs,
        out_specs=pl.BlockSpec((tq, w), lambda b, i: (b * nq + i, 0)),
        scratch_shapes=scratch,
        compiler_params=_cparams(("parallel", "arbitrary")),
        name="attn_c_lat" if ctx is not None else "attn_c_ctx",
    )(*args)


def _merge_kernel(x_ref, shift_ref, scale_ref, gate_ref, g_ref, wg_ref, oa_ref, ob_ref, oc_ref,
                  wba_ref, wbb_ref, wbc_ref, wo_ref, o_ref):
    x = x_ref[...]
    h = _norm_mod(x, g_ref, shift_ref, scale_ref).astype(BF16)
    mix = None
    for n, (br_ref, wb_ref) in enumerate(((oa_ref, wba_ref), (ob_ref, wbb_ref), (oc_ref, wbc_ref))):
        gate = _sigmoid(_dot(h, wg_ref[:, n * D:(n + 1) * D]))
        term = gate * _dot(br_ref[...], wb_ref[...])
        mix = term if mix is None else mix + term
    o_ref[...] = x + gate_ref[...] * _dot(mix.astype(BF16), wo_ref[...])


def _merge(x, mod4, norm_g3, wg, oa, ob, oc, wba, wbb, wbc, wo, l, row_of_tile):
    t = x.shape[0]
    row = lambda n: pl.BlockSpec((TM, n), lambda i: (i, 0))
    res = lambda r, c: pl.BlockSpec((None, r, c), lambda i: (l, 0, 0))
    return pl.pallas_call(
        _merge_kernel,
        out_shape=jax.ShapeDtypeStruct((t, D), F32),
        grid=(t // TM,),
        in_specs=[row(D), _mod_spec(l, 0, row_of_tile), _mod_spec(l, 1, row_of_tile),
                  _mod_spec(l, 2, row_of_tile), res(1, D), res(D, 3 * D),
                  row(A_HEADS * LANES), row(B_HEADS * LANES), row(C_HEADS * LANES),
                  res(A_HEADS * LANES, D), res(B_HEADS * LANES, D), res(C_HEADS * LANES, D), res(D, D)],
        out_specs=row(D),
        compiler_params=_cparams(("parallel",)),
        name="merge",
    )(x, mod4, mod4, mod4, norm_g3, wg, oa, ob, oc, wba, wbb, wbc, wo)


def _swiglu(h, wg, wu):
    a = _dot(h, wg)
    return (a * _sigmoid(a) * _dot(h, wu)).astype(BF16)


def _ffn_kernel(tf, x_ref, shift_ref, scale_ref, gate_ref, g_ref, wg_ref, wu_ref, wd_ref, o_ref):
    x = x_ref[...]
    h = _norm_mod(x, g_ref, shift_ref, scale_ref).astype(BF16)
    acc = None
    for c in range(wg_ref.shape[1] // tf):
        sl = slice(c * tf, (c + 1) * tf)
        part = _dot(_swiglu(h, wg_ref[:, sl], wu_ref[:, sl]), wd_ref[sl, :])
        acc = part if acc is None else acc + part
    o_ref[...] = x + gate_ref[...] * acc


def _ffn(x, mod4, norm_g3, wg, wu, wd, l, idx, row_of_tile):
    t = x.shape[0]
    dff = wg.shape[2]
    tf = 256
    assert dff % tf == 0
    row = pl.BlockSpec((TM, D), lambda i: (i, 0))
    return pl.pallas_call(
        functools.partial(_ffn_kernel, tf),
        out_shape=jax.ShapeDtypeStruct((t, D), F32),
        grid=(t // TM,),
        in_specs=[row, _mod_spec(l, 3, row_of_tile), _mod_spec(l, 4, row_of_tile),
                  _mod_spec(l, 5, row_of_tile),
                  pl.BlockSpec((None, 1, D), lambda i: (l, 0, 0)),
                  pl.BlockSpec((None, D, dff), lambda i: (idx, 0, 0)),
                  pl.BlockSpec((None, D, dff), lambda i: (idx, 0, 0)),
                  pl.BlockSpec((None, dff, D), lambda i: (idx, 0, 0))],
        out_specs=row,
        compiler_params=_cparams(("parallel",)),
        name="ffn",
    )(x, mod4, mod4, mod4, norm_g3, wg, wu, wd)


def _top2(lg, axis, n):
    idx = lax.broadcasted_iota(jnp.int32, lg.shape, axis)
    m1 = jnp.max(lg, axis=axis, keepdims=True)
    i1 = jnp.min(jnp.where(lg == m1, idx, n), axis=axis, keepdims=True)
    rest = jnp.where(idx == i1, NEG_INF, lg)
    m2 = jnp.max(rest, axis=axis, keepdims=True)
    i2 = jnp.min(jnp.where(rest == m2, idx, n), axis=axis, keepdims=True)
    return m1, i1, m2, i2


def _pack_halves(hb):
    lo = pltpu.bitcast(hb[:, :D // 2].astype(F32), jnp.uint32)
    hi = pltpu.bitcast(hb[:, D // 2:].astype(F32), jnp.uint32)
    return (lo >> 16) | hi


def _unpack_halves(u):
    lo = pltpu.bitcast(u << 16, F32).astype(BF16)
    hi = pltpu.bitcast(u & jnp.uint32(0xFFFF0000), F32).astype(BF16)
    return lo, hi


def _route_kernel(x_ref, shift_ref, scale_ref, g_ref, wr_ref, hp_ref, wts_ref, meta_ref, cnt_ref, run_s):
    @pl.when(pl.program_id(0) == 0)
    def _():
        run_s[...] = jnp.zeros_like(run_s)

    h = _norm_mod(x_ref[...], g_ref, shift_ref, scale_ref)
    hp_ref[...] = _pack_halves(h.astype(BF16))
    logits = jnp.dot(h, wr_ref[...], preferred_element_type=F32, precision=lax.Precision.HIGHEST)
    lane = lax.broadcasted_iota(jnp.int32, logits.shape, 1)
    m1, _, m2, _ = _top2(jnp.where(lane < N_EXPERTS, logits, NEG_INF), 1, LANES)
    e2 = jnp.exp(m2 - m1)
    den = 1.0 + e2
    wts_ref[...] = jnp.where(lane == 0, 1.0 / den, jnp.where(lane == 1, e2 / den, 0.0))

    lt = logits.T[:N_EXPERTS, :]
    _, i1, _, i2 = _top2(lt, 0, N_EXPERTS)
    sub = lax.broadcasted_iota(jnp.int32, lt.shape, 0)
    onehot = jnp.where((sub == i1) | (sub == i2), 1.0, 0.0)
    src = lax.broadcasted_iota(jnp.int32, (TM, TM), 0)
    dst = lax.broadcasted_iota(jnp.int32, (TM, TM), 1)
    earlier = jnp.where(src < dst, 1.0, 0.0).astype(BF16)
    before = _dot(onehot.astype(BF16), earlier) + run_s[:, 0:1]
    rank1 = jnp.sum(jnp.where(sub == i1, before, 0.0), axis=0, keepdims=True).astype(jnp.int32)
    rank2 = jnp.sum(jnp.where(sub == i2, before, 0.0), axis=0, keepdims=True).astype(jnp.int32)
    meta_ref[...] = jnp.where(sub == 0, i1, jnp.where(sub == 1, i2, jnp.where(sub == 2, rank1,
                              jnp.where(sub == 3, rank2, 0))))
    run_s[...] = run_s[...] + jnp.sum(onehot, axis=1, keepdims=True)
    cnt_ref[...] = run_s[...]


def _route(x, mod4, norm_g3, wr, l, idx, row_of_tile):
    t = x.shape[0]
    nt = t // TM
    return pl.pallas_call(
        _route_kernel,
        out_shape=[jax.ShapeDtypeStruct((t, D // 2), jnp.uint32), jax.ShapeDtypeStruct((t, LANES), F32),
                   jax.ShapeDtypeStruct((nt, N_EXPERTS, TM), jnp.int32),
                   jax.ShapeDtypeStruct((N_EXPERTS, LANES), F32)],
        grid=(nt,),
        in_specs=[pl.BlockSpec((TM, D), lambda i: (i, 0)),
                  _mod_spec(l, 3, row_of_tile), _mod_spec(l, 4, row_of_tile),
                  pl.BlockSpec((None, 1, D), lambda i: (l, 0, 0)),
                  pl.BlockSpec((None, D, LANES), lambda i: (idx, 0, 0))],
        out_specs=[pl.BlockSpec((TM, D // 2), lambda i: (i, 0)), pl.BlockSpec((TM, LANES), lambda i: (i, 0)),
                   pl.BlockSpec((None, N_EXPERTS, TM), lambda i: (i, 0, 0)),
                   pl.BlockSpec((N_EXPERTS, LANES), lambda i: (0, 0))],
        scratch_shapes=[pltpu.VMEM((N_EXPERTS, LANES), F32)],
        compiler_params=_cparams(("arbitrary",)),
        name="moe_route",
    )(x, mod4, mod4, norm_g3, wr)


def _row_copy(src, src_row, dst, dst_row, sem):
    return pltpu.make_async_copy(src.at[pl.ds(src_row, 1)], dst.at[pl.ds(dst_row, 1)], sem)


ROWS_PER_ISSUE = 8


def _for_each_assignment(fn):
    def trip(g, carry):
        row0 = pl.multiple_of(g * ROWS_PER_ISSUE, ROWS_PER_ISSUE)
        for j in range(ROWS_PER_ISSUE):
            for k in range(2):
                fn(row0 + j, k, 2 * row0 + (2 * j + k))
        return carry

    lax.fori_loop(0, TM // ROWS_PER_ISSUE, trip, 0)


def _scatter_kernel(dest_ref, hp_ref, xs_in_ref, xs_ref, sem):
    del xs_in_ref
    _for_each_assignment(lambda r, k, f: _row_copy(hp_ref, r, xs_ref, dest_ref[0, f], sem).start())
    _for_each_assignment(lambda r, k, f: _row_copy(hp_ref, 0, xs_ref, 0, sem).wait())


def _scatter(dest, hp, xs0):
    nt = dest.shape[0]
    return pl.pallas_call(
        _scatter_kernel,
        out_shape=jax.ShapeDtypeStruct(xs0.shape, xs0.dtype),
        grid=(nt,),
        in_specs=[pl.BlockSpec((None, 1, 2 * TM), lambda i: (i, 0, 0), memory_space=pltpu.SMEM),
                  pl.BlockSpec((TM, D // 2), lambda i: (i, 0)),
                  pl.BlockSpec(memory_space=pl.ANY)],
        out_specs=pl.BlockSpec(memory_space=pl.ANY),
        scratch_shapes=[pltpu.SemaphoreType.DMA(())],
        input_output_aliases={2: 0},
        compiler_params=_cparams(("arbitrary",)),
        name="moe_scatter",
    )(dest, hp, xs0)


def _experts_kernel(blk_in_ref, blk_e_ref, n_used_ref, xs_ref, wg_ref, wu_ref, wd_ref, ys_ref):
    del blk_in_ref, blk_e_ref
    used = pl.program_id(0) < n_used_ref[0]

    @pl.when(used)
    def _():
        lo, hi = _unpack_halves(xs_ref[...])
        half = D // 2
        a = _dot(lo, wg_ref[:half, :]) + _dot(hi, wg_ref[half:, :])
        b = _dot(lo, wu_ref[:half, :]) + _dot(hi, wu_ref[half:, :])
        ys_ref[...] = _dot((a * _sigmoid(a) * b).astype(BF16), wd_ref[...])

    @pl.when(jnp.logical_not(used))
    def _():
        ys_ref[...] = jnp.zeros_like(ys_ref)


def _experts(xs, tables, wg, wu, wd, idx):
    nblk = xs.shape[0] // BM
    dfe = wg.shape[3]
    grid_spec = pltpu.PrefetchScalarGridSpec(
        num_scalar_prefetch=3,
        grid=(nblk,),
        in_specs=[pl.BlockSpec((BM, D // 2), lambda i, bi, be, nu: (bi[i], 0)),
                  pl.BlockSpec((None, None, D, dfe), lambda i, bi, be, nu: (idx, be[i], 0, 0)),
                  pl.BlockSpec((None, None, D, dfe), lambda i, bi, be, nu: (idx, be[i], 0, 0)),
                  pl.BlockSpec((None, None, dfe, D), lambda i, bi, be, nu: (idx, be[i], 0, 0))],
        out_specs=pl.BlockSpec((BM, D), lambda i, bi, be, nu: (i, 0)),
    )
    return pl.pallas_call(
        _experts_kernel,
        out_shape=jax.ShapeDtypeStruct((nblk * BM, D), F32),
        grid_spec=grid_spec,
        compiler_params=_cparams(("arbitrary",)),
        name="moe_experts",
    )(*tables, xs, wg, wu, wd)


def _combine_kernel(dest_ref, x_ref, gate_ref, wts_ref, ys_ref, o_ref, buf, sem):
    _for_each_assignment(lambda r, k, f: _row_copy(ys_ref, dest_ref[0, f], buf.at[k], r, sem).start())
    _for_each_assignment(lambda r, k, f: _row_copy(ys_ref, 0, buf.at[k], 0, sem).wait())
    w = wts_ref[...]
    y = w[:, 0:1] * buf[0] + w[:, 1:2] * buf[1]
    o_ref[...] = x_ref[...] + gate_ref[...] * y


def _combine(dest, x, mod4, wts, ys, l, row_of_tile):
    t = x.shape[0]
    row = pl.BlockSpec((TM, D), lambda i: (i, 0))
    return pl.pallas_call(
        _combine_kernel,
        out_shape=jax.ShapeDtypeStruct((t, D), F32),
        grid=(t // TM,),
        in_specs=[pl.BlockSpec((None, 1, 2 * TM), lambda i: (i, 0, 0), memory_space=pltpu.SMEM),
                  row, _mod_spec(l, 5, row_of_tile),
                  pl.BlockSpec((TM, LANES), lambda i: (i, 0)),
                  pl.BlockSpec(memory_space=pl.ANY)],
        out_specs=row,
        scratch_shapes=[pltpu.VMEM((2, TM, D), F32), pltpu.SemaphoreType.DMA(())],
        compiler_params=_cparams(("arbitrary",)),
        name="moe_combine",
    )(dest, x, mod4, wts, ys)


def _moe(x, mod4, norm_g3, wr, wg, wu, wd, l, idx, row_of_tile):
    t = x.shape[0]
    nblk = 2 * t // BM + N_EXPERTS
    hp, wts, meta, cnt = _route(x, mod4, norm_g3, wr, l, idx, row_of_tile)
    counts = cnt[:, 0].astype(jnp.int32)
    blocks = (counts + BM - 1) // BM
    ends = jnp.cumsum(blocks)
    n_used = ends[-1]
    base = (ends - blocks) * BM
    experts = jnp.arange(N_EXPERTS, dtype=jnp.int32)

    def base_of(e):
        return jnp.sum(jnp.where(e[..., None] == experts, base, 0), axis=-1)

    dest = jnp.stack([base_of(meta[:, 0]) + meta[:, 2], base_of(meta[:, 1]) + meta[:, 3]], axis=-1)
    dest = dest.reshape(dest.shape[0], 1, 2 * TM)
    step = jnp.arange(nblk, dtype=jnp.int32)
    blk_in = jnp.minimum(step, jnp.maximum(n_used - 1, 0))
    blk_e = jnp.minimum(jnp.sum(blk_in[:, None] >= ends[None, :], axis=1), N_EXPERTS - 1).astype(jnp.int32)
    tables = (blk_in, blk_e, n_used.reshape(1))
    xs = _scatter(dest, hp, jnp.zeros((nblk * BM, D // 2), jnp.uint32))
    ys = _experts(xs, tables, wg, wu, wd, idx)
    return _combine(dest, x, mod4, wts, ys, l, row_of_tile)


def _final_kernel(x_ref, g_ref, o_ref):
    o_ref[...] = _rms(x_ref[...], g_ref[...])


def _final_norm(x, g2):
    t = x.shape[0]
    return pl.pallas_call(
        _final_kernel,
        out_shape=jax.ShapeDtypeStruct((t, D), F32),
        grid=(t // TM,),
        in_specs=[pl.BlockSpec((TM, D), lambda i: (i, 0)), pl.BlockSpec((1, D), lambda i: (0, 0))],
        out_specs=pl.BlockSpec((TM, D), lambda i: (i, 0)),
        compiler_params=_cparams(("parallel",)),
        name="final_norm",
    )(x, g2)


def _relayout_w_in(w_in):
    depth = w_in.shape[0]
    z = lambda n: jnp.zeros((depth, D, n), w_in.dtype)
    aq, ak, av = w_in[..., 0:512], w_in[..., 512:1024], w_in[..., 1024:1536]
    bq, bk, bv = w_in[..., 1536:2048], w_in[..., 2048:2176], w_in[..., 2176:2304]
    cq, ckv, ckr = w_in[..., 2304:2560], w_in[..., 2560:2688], w_in[..., 2688:2720]
    gates = w_in[..., 2720:2720 + 3 * D]
    pieces = [aq, ak, av]
    for h in range(B_HEADS):
        qh = bq[..., h * HD:(h + 1) * HD]
        pieces += [qh, z(HD)] if h // B_GROUP == 0 else [z(HD), qh]
    pieces += [bk, bv, cq, ckv, z(KR_OFF), ckr, z(LANES - KR_OFF - C_ROPE)]
    w_a = jnp.concatenate(pieces, axis=-1).astype(BF16)
    assert w_a.shape[-1] == NPA
    return w_a, gates.astype(BF16)


def _relayout_mla(w_q_up, w_kv_up):
    depth = w_q_up.shape[0]
    qd = C_NOPE + C_ROPE
    wq = jnp.pad(w_q_up.reshape(depth, C_Q_LORA, C_HEADS, qd), ((0, 0), (0, 0), (0, 0), (0, LANES - qd)))
    kv = w_kv_up.reshape(depth, C_KV_LORA, C_HEADS, C_NOPE + C_VDIM)
    wkn = jnp.pad(kv[..., :C_NOPE], ((0, 0), (0, 0), (0, 0), (0, LANES - C_NOPE)))
    wv = jnp.pad(kv[..., C_NOPE:], ((0, 0), (0, 0), (0, 0), (0, LANES - C_VDIM)))
    flat = lambda w: w.reshape(depth, w.shape[1], C_HEADS * LANES).astype(BF16)
    return flat(wq), flat(wkn), flat(wv)


def _relayout_w_branch(w_branch):
    depth = w_branch.shape[0]
    wba = w_branch[:, 0]
    b = w_branch[:, 1].reshape(depth, B_HEADS, HD, D)
    zb = jnp.zeros_like(b)
    lo = jnp.concatenate([b, zb], axis=2)
    hi = jnp.concatenate([zb, b], axis=2)
    grp = (jnp.arange(B_HEADS) // B_GROUP).reshape(1, B_HEADS, 1, 1)
    wbb = jnp.where(grp == 0, lo, hi).reshape(depth, B_HEADS * LANES, D)
    c = w_branch[:, 2].reshape(depth, C_HEADS, C_VDIM, D)
    wbc = jnp.pad(c, ((0, 0), (0, 0), (0, LANES - C_VDIM), (0, 0))).reshape(depth, C_HEADS * LANES, D)
    return wba.astype(BF16), wbb.astype(BF16), wbc.astype(BF16)


def _rope_tables(n_tokens, rot_dim, lane_off, group):
    rows_n = n_tokens // GRID_W
    row, col = jnp.meshgrid(jnp.arange(rows_n), jnp.arange(GRID_W), indexing="ij")
    row = row.reshape(-1).astype(F32)
    col = col.reshape(-1).astype(F32)
    n_freq = rot_dim // 4
    inv = jnp.power(ROPE_THETA, -jnp.arange(n_freq, dtype=F32) / n_freq)
    ang = jnp.concatenate([row[:, None] * inv, col[:, None] * inv], axis=-1)
    cos, sin = jnp.cos(ang), jnp.sin(ang)
    half = rot_dim // 2
    c = jnp.ones((n_tokens, LANES), F32)
    sa = jnp.zeros((n_tokens, LANES), F32)
    sb = jnp.zeros((n_tokens, LANES), F32)
    for off in range(lane_off, LANES - rot_dim + 1, group):
        c = c.at[:, off:off + half].set(cos).at[:, off + half:off + rot_dim].set(cos)
        sa = sa.at[:, off:off + half].set(-sin)
        sb = sb.at[:, off + half:off + rot_dim].set(sin)
        if group >= LANES:
            break
    ident = (jnp.ones((TM, LANES), F32), jnp.zeros((TM, LANES), F32), jnp.zeros((TM, LANES), F32))
    return tuple(jnp.concatenate([a, i], axis=0) for a, i in zip((c, sa, sb), ident))


def _run_group(x, nb, s, row_of_tile, tab_block, tabs, caches, mod4, w):
    depth = w["w_a"].shape[0]
    t = nb * s
    assert t % TM == 0 and (s % TM == 0 or TM % s == 0)
    pas, ckvns = [], []
    for l in range(depth):
        qa, ka, va, qb, kb, vb, qc, kc, vc, ckvn, *raw = _proj(
            x, mod4, w["norm_attn_g"], w["w_a"], tabs, w["gq"], w["gkv"], w["wq"], w["wkn"], w["wv"],
            l, row_of_tile, tab_block, caches is None)
        if caches is None:
            ctx_a = ctx_b = ctx_c = None
            kv_a, kv_b, kr = raw
            aw = A_HEADS * LANES
            pas.append((kv_a[:, :aw], kv_a[:, aw:], kv_b[:, :LANES], kv_b[:, LANES:],
                        kr[:, KR_OFF:KR_OFF + C_ROPE]))
            ckvns.append(ckvn)
        else:
            ctx_a, ctx_b, ctx_c = caches
        oa = _attn_a(qa, ka, va, ctx_a, w["lam4"], w["subg"], l, nb, s)
        ob = _attn_b(qb, kb, vb, ctx_b, w["win_sink"], l, nb, s)
        oc = _attn_c(qc, kc, vc, ctx_c, w["wkn"], w["wv"], l, nb, s)
        x = _merge(x, mod4, w["norm_attn_g"], w["w_gates"], oa, ob, oc,
                   w["wba"], w["wbb"], w["wbc"], w["w_out"], l, row_of_tile)
        if l % 2 == 0:
            x = _ffn(x, mod4, w["norm_ffn_g"], w["ffn_g"], w["ffn_u"], w["ffn_d"], l, l // 2, row_of_tile)
        else:
            x = _moe(x, mod4, w["norm_ffn_g"], w["moe_r"], w["moe_g"], w["moe_u"], w["moe_d"],
                     l, l // 2, row_of_tile)
    return _final_norm(x, w["final_g"]), pas, ckvns


def kernel(x_prompt, x_sample, cache_diff_k, cache_diff_v, cache_win_k, cache_win_v, cache_mla_ckv, cache_mla_krope, c, c_ctx, w_ada, b_ada, norm_attn_g, norm_ffn_g, w_in, diff_lambda_q1, diff_lambda_k1, diff_lambda_q2, diff_lambda_k2, diff_subln_g, win_sink, mla_q_norm_g, mla_w_q_up, mla_kv_norm_g, mla_w_kv_up, w_branch, w_out, ffn_w_gate, ffn_w_up, ffn_w_down, moe_w_router, moe_w_gate, moe_w_up, moe_w_down, final_norm_g):
    depth = w_in.shape[0]
    nbc, sc, _ = x_prompt.shape
    nbl, sl, _ = x_sample.shape
    past = cache_diff_k.shape[2]
    assert 1 + nbl <= MOD_ROWS

    cond = jnp.zeros((MOD_ROWS, D), F32).at[0].set(c_ctx).at[1:1 + nbl].set(c)
    mod4 = _ada_mod(cond, w_ada, b_ada).reshape(depth, MOD_ROWS, 1, 6 * D)

    w_a, w_gates = _relayout_w_in(w_in)
    wq, wkn, wv = _relayout_mla(mla_w_q_up, mla_w_kv_up)
    wba, wbb, wbc = _relayout_w_branch(w_branch)
    vec3 = lambda a: a.reshape(a.shape[0], 1, a.shape[1])
    w = dict(
        w_a=w_a, w_gates=w_gates, wq=wq, wkn=wkn, wv=wv, wba=wba, wbb=wbb, wbc=wbc,
        w_out=w_out.astype(BF16), norm_attn_g=vec3(norm_attn_g), norm_ffn_g=vec3(norm_ffn_g),
        gq=vec3(mla_q_norm_g), gkv=vec3(mla_kv_norm_g), subg=vec3(diff_subln_g),
        lam4=[vec3(a) for a in (diff_lambda_q1, diff_lambda_k1, diff_lambda_q2, diff_lambda_k2)],
        win_sink=win_sink,
        ffn_g=ffn_w_gate.astype(BF16), ffn_u=ffn_w_up.astype(BF16), ffn_d=ffn_w_down.astype(BF16),
        moe_r=jnp.pad(moe_w_router, ((0, 0), (0, 0), (0, LANES - N_EXPERTS))),
        moe_g=moe_w_gate.astype(BF16), moe_u=moe_w_up.astype(BF16), moe_d=moe_w_down.astype(BF16),
        final_g=final_norm_g.reshape(1, D),
    )

    tabs_h = _rope_tables(sl, HD, 0, HD)
    tabs_r = _rope_tables(sl, C_ROPE, KR_OFF, LANES)
    tabs = tabs_h + tabs_r
    ident_block = sl // TM

    y_c, pas, ckvns = _run_group(
        x_prompt.reshape(nbc * sc, D), nbc, sc, lambda i: 0, lambda i: ident_block, tabs, None, mod4, w)

    tiles_per_b = sl // TM
    krp = jnp.pad(cache_mla_krope, ((0, 0), (0, 0), (0, 0), (KR_OFF, LANES - KR_OFF - C_ROPE)))
    caches = (
        (cache_diff_k.reshape(nbl, depth, past, A_HEADS * LANES),
         cache_diff_v.reshape(nbl, depth, past, A_HEADS * LANES)),
        (cache_win_k.reshape(nbl, depth, past, LANES), cache_win_v.reshape(nbl, depth, past, LANES)),
        (cache_mla_ckv, krp),
    )
    y_l, _, _ = _run_group(
        x_sample.reshape(nbl * sl, D), nbl, sl, lambda i: 1 + i // tiles_per_b,
        lambda i: i % tiles_per_b, tabs, caches, mod4, w)

    def cache(per_layer, tail):
        return jnp.stack([a.reshape((nbc, sc) + tail) for a in per_layer], axis=1)

    new_diff_k = cache([p[0] for p in pas], (A_HEADS, 2 * HD))
    new_diff_v = cache([p[1] for p in pas], (A_HEADS, 2 * HD))
    new_win_k = cache([p[2] for p in pas], (B_HEADS // B_GROUP, HD))
    new_win_v = cache([p[3] for p in pas], (B_HEADS // B_GROUP, HD))
    new_mla_ckv = cache(ckvns, (C_KV_LORA,))
    new_mla_krope = cache([p[4] for p in pas], (C_ROPE,))
    return (y_c.reshape(nbc, sc, D), y_l.reshape(nbl, sl, D), new_diff_k, new_diff_v,
            new_win_k, new_win_v, new_mla_ckv, new_mla_krope)
```

```python
import functools
import math

import jax
import jax.numpy as jnp
from jax import lax
from jax.experimental import pallas as pl
from jax.experimental.pallas import tpu as pltpu

F32 = jnp.float32
BF16 = jnp.bfloat16

D = 1024
HD = 64
LANES = 128
GRID_W = 64
BLOCK = 128
WINDOW = 128
A_HEADS = 4
B_HEADS = 8
B_GROUP = 4
C_HEADS = 8
C_Q_LORA = 256
C_KV_LORA = 128
C_NOPE = 64
C_ROPE = 32
C_VDIM = 64
N_EXPERTS = 8
ROPE_THETA = 10000.0
EPS = 1e-6
NEG_INF = -1e30
LOG2E = math.log2(math.e)
MOD_ROWS = 16

TM = 512
TQ = 256
TQC = 512
TQB = 512
BM = 512
VMEM_LIMIT = 56 * 1024 * 1024

CH_AQ, CH_AK, CH_AV, CH_BQ, CH_BK, CH_BV, CH_CQ, CH_CKV, CH_CKR = 0, 4, 8, 12, 20, 21, 22, 24, 25
N_CH = 26
NPA = N_CH * LANES
KR_OFF = C_NOPE


def _cparams(sem):
    return pltpu.CompilerParams(dimension_semantics=sem, vmem_limit_bytes=VMEM_LIMIT)


def _rms(x, g):
    return x * lax.rsqrt(jnp.mean(x * x, axis=-1, keepdims=True) + EPS) * g


def _sigmoid(x):
    return 1.0 / (1.0 + jnp.exp(-x))


def _dot(a, b):
    return jnp.dot(a, b, preferred_element_type=F32)


def _dot_t(a, b):
    return lax.dot_general(a, b, (((1,), (1,)), ((), ())), preferred_element_type=F32)


def _ada_kernel(c_ref, w_ref, b_ref, o_ref):
    c = c_ref[...]
    s = (c * _sigmoid(c)).astype(BF16)
    o_ref[...] = _dot(s, w_ref[...].astype(BF16)) + b_ref[...]


def _ada_mod(cond, w_ada, b_ada):
    depth, _, n = w_ada.shape
    tn = 1536
    return pl.pallas_call(
        _ada_kernel,
        out_shape=jax.ShapeDtypeStruct((depth, MOD_ROWS, n), F32),
        grid=(depth, n // tn),
        in_specs=[
            pl.BlockSpec((MOD_ROWS, D), lambda l, j: (0, 0)),
            pl.BlockSpec((None, D, tn), lambda l, j: (l, 0, j)),
            pl.BlockSpec((None, 1, tn), lambda l, j: (l, 0, j)),
        ],
        out_specs=pl.BlockSpec((None, MOD_ROWS, tn), lambda l, j: (l, 0, j)),
        compiler_params=_cparams(("parallel", "parallel")),
        name="ada_mod",
    )(cond, w_ada, b_ada.reshape(depth, 1, n))


def _mod_spec(l, k, row_of_tile):
    return pl.BlockSpec((None, None, 1, D), lambda i: (l, row_of_tile(i), 0, k))


def _norm_mod(x, g_ref, shift_ref, scale_ref):
    return _rms(x, g_ref[...]) * (1.0 + scale_ref[...]) + shift_ref[...]


def _rope(x, c, sa, sb, half):
    return x * c + pltpu.roll(x, LANES - half, 1) * sa + pltpu.roll(x, half, 1) * sb


def _proj_kernel(emit_cache, x_ref, shift_ref, scale_ref, g_ref, w_ref,
                 hc_ref, hsa_ref, hsb_ref, rc_ref, rsa_ref, rsb_ref, gq_ref, gkv_ref,
                 wq_ref, wkn_ref, wv_ref,
                 qa_ref, ka_ref, va_ref, qb_ref, kb_ref, vb_ref, qc_ref, kc_ref, vc_ref, ckvn_ref,
                 *cache_refs):
    h = _norm_mod(x_ref[...], g_ref, shift_ref, scale_ref).astype(BF16)
    pa = _dot(h, w_ref[...])
    hc, hsa, hsb = hc_ref[...], hsa_ref[...], hsb_ref[...]
    rc, rsa, rsb = rc_ref[...], rsa_ref[...], rsb_ref[...]

    def chunk(c, n=1):
        return pa[:, c * LANES:(c + n) * LANES]

    def put(ref, c, val):
        ref[:, c * LANES:(c + 1) * LANES] = val.astype(ref.dtype)

    if emit_cache:
        dk_ref, dv_ref, kv_b_ref, kr_ref = cache_refs
        for hd in range(A_HEADS):
            dk_ref[:, hd, :] = chunk(CH_AK + hd)
            dv_ref[:, hd, :] = chunk(CH_AV + hd)
        kv_b_ref[...] = chunk(CH_BK, 2)
        kr_ref[...] = chunk(CH_CKR)

    q_scale = HD ** -0.5
    for c in range(A_HEADS):
        put(qa_ref, c, _rope(chunk(CH_AQ + c), hc, hsa, hsb, HD // 2) * (q_scale * LOG2E))
        put(ka_ref, c, _rope(chunk(CH_AK + c), hc, hsa, hsb, HD // 2))
        put(va_ref, c, chunk(CH_AV + c))
    for c in range(B_HEADS):
        put(qb_ref, c, _rope(chunk(CH_BQ + c), hc, hsa, hsb, HD // 2) * (q_scale * LOG2E))
    put(kb_ref, 0, _rope(chunk(CH_BK), hc, hsa, hsb, HD // 2))
    put(vb_ref, 0, chunk(CH_BV))

    cqn = _rms(chunk(CH_CQ, 2), gq_ref[...]).astype(BF16)
    qup = _dot(cqn, wq_ref[...])
    c_scale = (C_NOPE + C_ROPE) ** -0.5 * LOG2E
    for c in range(C_HEADS):
        put(qc_ref, c, _rope(qup[:, c * LANES:(c + 1) * LANES], rc, rsa, rsb, C_ROPE // 2) * c_scale)

    ckvn = _rms(chunk(CH_CKV), gkv_ref[...])
    ckvn_ref[...] = ckvn
    ckvb = ckvn.astype(BF16)
    kn = _dot(ckvb, wkn_ref[...])
    kr = _rope(chunk(CH_CKR), rc, rsa, rsb, C_ROPE // 2)
    for c in range(C_HEADS):
        put(kc_ref, c, kn[:, c * LANES:(c + 1) * LANES] + kr)
    vc_ref[...] = _with_ones_lane(_dot(ckvb, wv_ref[...])).astype(BF16)


def _proj(x, mod4, norm_g3, w_a, tabs, gq3, gkv3, wq, wkn, wv, l, row_of_tile, tab_block, emit_cache):
    t = x.shape[0]
    tab_spec = pl.BlockSpec((TM, LANES), lambda i: (tab_block(i), 0))
    def wide(n):
        if isinstance(n, tuple):
            return pl.BlockSpec((TM,) + n, lambda i: (i, 0, 0))
        return pl.BlockSpec((TM, n), lambda i: (i, 0))

    outs = [(4 * LANES, BF16)] * 3 + [(8 * LANES, BF16), (LANES, BF16), (LANES, BF16)] \
        + [(8 * LANES, BF16)] * 3 + [(LANES, F32)]
    if emit_cache:
        outs += [((A_HEADS, LANES), F32)] * 2 + [(2 * LANES, F32), (LANES, F32)]
    shape = lambda n: (t,) + n if isinstance(n, tuple) else (t, n)
    return pl.pallas_call(
        functools.partial(_proj_kernel, emit_cache),
        out_shape=[jax.ShapeDtypeStruct(shape(n), dt) for n, dt in outs],
        grid=(t // TM,),
        in_specs=[
            wide(D), _mod_spec(l, 0, row_of_tile), _mod_spec(l, 1, row_of_tile),
            pl.BlockSpec((None, 1, D), lambda i: (l, 0, 0)),
            pl.BlockSpec((None, D, NPA), lambda i: (l, 0, 0)),
        ] + [tab_spec] * 6 + [
            pl.BlockSpec((None, 1, C_Q_LORA), lambda i: (l, 0, 0)),
            pl.BlockSpec((None, 1, C_KV_LORA), lambda i: (l, 0, 0)),
            pl.BlockSpec((None, C_Q_LORA, C_HEADS * LANES), lambda i: (l, 0, 0)),
            pl.BlockSpec((None, C_KV_LORA, C_HEADS * LANES), lambda i: (l, 0, 0)),
            pl.BlockSpec((None, C_KV_LORA, C_HEADS * LANES), lambda i: (l, 0, 0)),
        ],
        out_specs=[wide(n) for n, _ in outs],
        compiler_params=_cparams(("parallel",)),
        name="proj",
    )(x, mod4, mod4, norm_g3, w_a, *tabs, gq3, gkv3, wq, wkn, wv)


def _exp2_parts(scores):
    m = functools.reduce(jnp.maximum, [jnp.max(s, axis=-1, keepdims=True) for s in scores])
    es = [jnp.exp2(s - m) for s in scores]
    den = functools.reduce(lambda a, b: a + b, [jnp.sum(e, axis=-1, keepdims=True) for e in es])
    return es, den


def _attn_a_kernel(lam_init, has_ctx, q_ref, k_ref, v_ref, *rest):
    if has_ctx:
        kc_ref, vc_ref, lq1_ref, lk1_ref, lq2_ref, lk2_ref, sg_ref, o_ref = rest
    else:
        lq1_ref, lk1_ref, lq2_ref, lk2_ref, sg_ref, o_ref = rest
    lam = (jnp.exp(jnp.sum(lq1_ref[...] * lk1_ref[...], axis=-1, keepdims=True))
           - jnp.exp(jnp.sum(lq2_ref[...] * lk2_ref[...], axis=-1, keepdims=True)) + lam_init)
    lane = lax.broadcasted_iota(jnp.int32, (q_ref.shape[0], LANES), 1)
    for h in range(A_HEADS):
        sl = slice(h * LANES, (h + 1) * LANES)
        q = q_ref[:, sl]
        zero = jnp.zeros_like(q)
        q1 = jnp.where(lane < HD, q, zero)
        q2 = jnp.where(lane >= HD, q, zero)
        keys, vals = [k_ref[:, sl]], [v_ref[:, sl]]
        if has_ctx:
            keys.append(kc_ref[:, sl].astype(BF16))
            vals.append(vc_ref[:, sl].astype(BF16))
        e1, l1 = _exp2_parts([_dot_t(q1, k) for k in keys])
        e2, l2 = _exp2_parts([_dot_t(q2, k) for k in keys])
        c = lam * l1 / l2
        o = None
        for a1, a2, v in zip(e1, e2, vals):
            part = _dot((a1 - c * a2).astype(BF16), v)
            o = part if o is None else o + part
        o = o * (1.0 / l1)
        o_ref[:, sl] = (_rms(o, sg_ref[...]) * (1.0 - lam_init)).astype(o_ref.dtype)


def _attn_a(qa, ka, va, ctx, lam4, subg3, l, nb, s):
    t = qa.shape[0]
    tq = min(TQ, s)
    nq = s // tq
    w = A_HEADS * LANES
    lam_init = 0.8 - 0.6 * math.exp(-0.3 * l)
    in_specs = [
        pl.BlockSpec((tq, w), lambda b, i: (b * nq + i, 0)),
        pl.BlockSpec((s, w), lambda b, i: (b, 0)),
        pl.BlockSpec((s, w), lambda b, i: (b, 0)),
    ]
    args = [qa, ka, va]
    if ctx is not None:
        past = ctx[0].shape[2]
        in_specs += [pl.BlockSpec((None, None, past, w), lambda b, i: (b, l, 0, 0))] * 2
        args += list(ctx)
    in_specs += [pl.BlockSpec((None, 1, HD), lambda b, i: (l, 0, 0))] * 4
    in_specs += [pl.BlockSpec((None, 1, LANES), lambda b, i: (l, 0, 0))]
    return pl.pallas_call(
        functools.partial(_attn_a_kernel, lam_init, ctx is not None),
        out_shape=jax.ShapeDtypeStruct((t, w), BF16),
        grid=(nb, nq),
        in_specs=in_specs,
        out_specs=pl.BlockSpec((tq, w), lambda b, i: (b * nq + i, 0)),
        compiler_params=_cparams(("parallel", "arbitrary")),
        name="attn_a_lat" if ctx is not None else "attn_a_ctx",
    )(*args, *lam4, subg3)


def _sink_attend(qs, keys, vals, biases, sinkv):
    scores = []
    for k, bias in zip(keys, biases):
        s = _dot_t(qs, k)
        scores.append(s if bias is None else s + bias)
    m = functools.reduce(jnp.maximum, [jnp.max(s, axis=-1, keepdims=True) for s in scores] + [sinkv])
    es = [jnp.exp2(s - m) for s in scores]
    den = functools.reduce(lambda a, b: a + b,
                           [jnp.sum(e, axis=-1, keepdims=True) for e in es] + [jnp.exp2(sinkv - m)])
    o = None
    for e, v in zip(es, vals):
        part = _dot(e.astype(BF16), v)
        o = part if o is None else o + part
    return o * (1.0 / den)


def _stack_group(q_ref, row0, rows, g):
    return jnp.concatenate(
        [q_ref[pl.ds(row0, rows), (B_GROUP * g + j) * LANES:(B_GROUP * g + j + 1) * LANES]
         for j in range(B_GROUP)], axis=0)


def _sink_column(sink_ref, l, g, rows):
    row = lax.broadcasted_iota(jnp.int32, (B_GROUP * rows, 1), 0)
    col = jnp.full((B_GROUP * rows, 1), sink_ref[l, B_GROUP * g + B_GROUP - 1], F32)
    for j in range(B_GROUP - 2, -1, -1):
        col = jnp.where(row < (j + 1) * rows, sink_ref[l, B_GROUP * g + j], col)
    return col * LOG2E


def _store_group(o_ref, row0, rows, g, o):
    lane = lax.broadcasted_iota(jnp.int32, o.shape, 1)
    o = jnp.where((lane >= g * HD) & (lane < (g + 1) * HD), o, 0.0).astype(o_ref.dtype)
    for j in range(B_GROUP):
        c = B_GROUP * g + j
        o_ref[pl.ds(row0, rows), c * LANES:(c + 1) * LANES] = o[j * rows:(j + 1) * rows, :]


def _attn_b_ctx_kernel(l, sink_ref, q_ref, k_ref, v_ref, o_ref):
    rows = q_ref.shape[0]
    k, v = k_ref[...], v_ref[...]
    for g in range(B_HEADS // B_GROUP):
        qs = _stack_group(q_ref, 0, rows, g)
        o = _sink_attend(qs, [k], [v], [None], _sink_column(sink_ref, l, g, rows))
        _store_group(o_ref, 0, rows, g, o)


def _window_bias():
    q = jnp.arange(B_GROUP * BLOCK)[:, None] % BLOCK
    k = jnp.arange(3 * BLOCK)[None, :]
    d = jnp.arange(3)[:, None, None] * BLOCK
    return jnp.where(jnp.abs(d + q - k) <= WINDOW, 0.0, NEG_INF).astype(F32)


def _attn_b_lat_kernel(l, s_len, sink_ref, q_ref, k_ref, v_ref, kc_ref, vc_ref, bias_ref, o_ref):
    qi = pl.program_id(1)
    kc = kc_ref[...].astype(BF16)
    vc = vc_ref[...].astype(BF16)
    win = 3 * BLOCK
    for n in range(TQB // BLOCK):
        blk = qi * (TQB // BLOCK) + n
        first = jnp.clip(blk - 1, 0, s_len // BLOCK - 3)
        ws = pl.multiple_of(first * BLOCK, BLOCK)
        kw = k_ref[pl.ds(ws, win), :]
        vw = v_ref[pl.ds(ws, win), :]
        bias = bias_ref[blk - first]
        for g in range(B_HEADS // B_GROUP):
            qs = _stack_group(q_ref, n * BLOCK, BLOCK, g)
            o = _sink_attend(qs, [kw, kc], [vw, vc], [bias, None], _sink_column(sink_ref, l, g, BLOCK))
            _store_group(o_ref, n * BLOCK, BLOCK, g, o)


def _attn_b(qb, kb, vb, ctx, win_sink, l, nb, s):
    t = qb.shape[0]
    smem = pl.BlockSpec(memory_space=pltpu.SMEM)
    if ctx is None:
        return pl.pallas_call(
            functools.partial(_attn_b_ctx_kernel, l),
            out_shape=jax.ShapeDtypeStruct((t, B_HEADS * LANES), BF16),
            grid=(nb,),
            in_specs=[smem,
                      pl.BlockSpec((s, B_HEADS * LANES), lambda b: (b, 0)),
                      pl.BlockSpec((s, LANES), lambda b: (b, 0)),
                      pl.BlockSpec((s, LANES), lambda b: (b, 0))],
            out_specs=pl.BlockSpec((s, B_HEADS * LANES), lambda b: (b, 0)),
            compiler_params=_cparams(("parallel",)),
            name="attn_b_ctx",
        )(win_sink, qb, kb, vb)
    assert s % TQB == 0 and s >= 3 * BLOCK
    nq = s // TQB
    past = ctx[0].shape[2]
    return pl.pallas_call(
        functools.partial(_attn_b_lat_kernel, l, s),
        out_shape=jax.ShapeDtypeStruct((t, B_HEADS * LANES), BF16),
        grid=(nb, nq),
        in_specs=[smem,
                  pl.BlockSpec((TQB, B_HEADS * LANES), lambda b, i: (b * nq + i, 0)),
                  pl.BlockSpec((s, LANES), lambda b, i: (b, 0)),
                  pl.BlockSpec((s, LANES), lambda b, i: (b, 0)),
                  pl.BlockSpec((None, None, past, LANES), lambda b, i: (b, l, 0, 0)),
                  pl.BlockSpec((None, None, past, LANES), lambda b, i: (b, l, 0, 0)),
                  pl.BlockSpec((3, B_GROUP * BLOCK, 3 * BLOCK), lambda b, i: (0, 0, 0))],
        out_specs=pl.BlockSpec((TQB, B_HEADS * LANES), lambda b, i: (b * nq + i, 0)),
        compiler_params=_cparams(("parallel", "arbitrary")),
        name="attn_b_lat",
    )(win_sink, qb, kb, vb, *ctx, _window_bias())


def _with_ones_lane(v):
    lane = lax.broadcasted_iota(jnp.int32, v.shape, 1)
    return jnp.where(lane % LANES == C_VDIM, 1.0, v)


def _attn_c_kernel(has_ctx, q_ref, k_ref, v_ref, *rest):
    if has_ctx:
        ckv_ref, krp_ref, wkn_ref, wv_ref, o_ref, kc_s, vc_s = rest

        @pl.when(pl.program_id(1) == 0)
        def _():
            ckv = ckv_ref[...].astype(BF16)
            kn = _dot(ckv, wkn_ref[...])
            krp = krp_ref[...]
            for h in range(C_HEADS):
                sl = slice(h * LANES, (h + 1) * LANES)
                kc_s[:, sl] = (kn[:, sl] + krp).astype(BF16)
            vc_s[...] = _with_ones_lane(_dot(ckv, wv_ref[...])).astype(BF16)
    else:
        (o_ref,) = rest
    for h in range(C_HEADS):
        sl = slice(h * LANES, (h + 1) * LANES)
        q = q_ref[:, sl]
        keys, vals = [k_ref[:, sl]], [v_ref[:, sl]]
        if has_ctx:
            keys.append(kc_s[:, sl])
            vals.append(vc_s[:, sl])
        scores = [_dot_t(q, k) for k in keys]
        m = functools.reduce(jnp.maximum, [jnp.max(s, axis=-1, keepdims=True) for s in scores])
        o = None
        for s, v in zip(scores, vals):
            part = _dot(jnp.exp2(s - m).astype(BF16), v)
            o = part if o is None else o + part
        o_ref[:, sl] = (o * (1.0 / o[:, C_VDIM:C_VDIM + 1])).astype(o_ref.dtype)


def _attn_c(qc, kc, vc, ctx, wkn, wv, l, nb, s):
    t = qc.shape[0]
    tq = min(TQC, s)
    nq = s // tq
    w = C_HEADS * LANES
    in_specs = [
        pl.BlockSpec((tq, w), lambda b, i: (b * nq + i, 0)),
        pl.BlockSpec((s, w), lambda b, i: (b, 0)),
        pl.BlockSpec((s, w), lambda b, i: (b, 0)),
    ]
    args = [qc, kc, vc]
    scratch = []
    if ctx is not None:
        past = ctx[0].shape[2]
        in_specs += [pl.BlockSpec((None, None, past, LANES), lambda b, i: (b, l, 0, 0))] * 2
        in_specs += [pl.BlockSpec((None, C_KV_LORA, w), lambda b, i: (l, 0, 0))] * 2
        args += list(ctx) + [wkn, wv]
        scratch = [pltpu.VMEM((past, w), BF16)] * 2
    return pl.pallas_call(
        functools.partial(_attn_c_kernel, ctx is not None),
        out_shape=jax.ShapeDtypeStruct((t, w), BF16),
        grid=(nb, nq),
        in_specs=in_specs,
        out_specs=pl.BlockSpec((tq, w), lambda b, i: (b * nq + i, 0)),
        scratch_shapes=scratch,
        compiler_params=_cparams(("parallel", "arbitrary")),
        name="attn_c_lat" if ctx is not None else "attn_c_ctx",
    )(*args)


def _merge_kernel(x_ref, shift_ref, scale_ref, gate_ref, g_ref, wg_ref, oa_ref, ob_ref, oc_ref,
                  wba_ref, wbb_ref, wbc_ref, wo_ref, o_ref):
    x = x_ref[...]
    h = _norm_mod(x, g_ref, shift_ref, scale_ref).astype(BF16)
    mix = None
    for n, (br_ref, wb_ref) in enumerate(((oa_ref, wba_ref), (ob_ref, wbb_ref), (oc_ref, wbc_ref))):
        gate = _sigmoid(_dot(h, wg_ref[:, n * D:(n + 1) * D]))
        term = gate * _dot(br_ref[...], wb_ref[...])
        mix = term if mix is None else mix + term
    o_ref[...] = x + gate_ref[...] * _dot(mix.astype(BF16), wo_ref[...])


def _merge(x, mod4, norm_g3, wg, oa, ob, oc, wba, wbb, wbc, wo, l, row_of_tile):
    t = x.shape[0]
    row = lambda n: pl.BlockSpec((TM, n), lambda i: (i, 0))
    res = lambda r, c: pl.BlockSpec((None, r, c), lambda i: (l, 0, 0))
    return pl.pallas_call(
        _merge_kernel,
        out_shape=jax.ShapeDtypeStruct((t, D), F32),
        grid=(t // TM,),
        in_specs=[row(D), _mod_spec(l, 0, row_of_tile), _mod_spec(l, 1, row_of_tile),
                  _mod_spec(l, 2, row_of_tile), res(1, D), res(D, 3 * D),
                  row(A_HEADS * LANES), row(B_HEADS * LANES), row(C_HEADS * LANES),
                  res(A_HEADS * LANES, D), res(B_HEADS * LANES, D), res(C_HEADS * LANES, D), res(D, D)],
        out_specs=row(D),
        compiler_params=_cparams(("parallel",)),
        name="merge",
    )(x, mod4, mod4, mod4, norm_g3, wg, oa, ob, oc, wba, wbb, wbc, wo)


def _swiglu(h, wg, wu):
    a = _dot(h, wg)
    return (a * _sigmoid(a) * _dot(h, wu)).astype(BF16)


def _ffn_kernel(tf, x_ref, shift_ref, scale_ref, gate_ref, g_ref, wg_ref, wu_ref, wd_ref, o_ref):
    x = x_ref[...]
    h = _norm_mod(x, g_ref, shift_ref, scale_ref).astype(BF16)
    acc = None
    for c in range(wg_ref.shape[1] // tf):
        sl = slice(c * tf, (c + 1) * tf)
        part = _dot(_swiglu(h, wg_ref[:, sl], wu_ref[:, sl]), wd_ref[sl, :])
        acc = part if acc is None else acc + part
    o_ref[...] = x + gate_ref[...] * acc


def _ffn(x, mod4, norm_g3, wg, wu, wd, l, idx, row_of_tile):
    t = x.shape[0]
    dff = wg.shape[2]
    tf = 256
    assert dff % tf == 0
    row = pl.BlockSpec((TM, D), lambda i: (i, 0))
    return pl.pallas_call(
        functools.partial(_ffn_kernel, tf),
        out_shape=jax.ShapeDtypeStruct((t, D), F32),
        grid=(t // TM,),
        in_specs=[row, _mod_spec(l, 3, row_of_tile), _mod_spec(l, 4, row_of_tile),
                  _mod_spec(l, 5, row_of_tile),
                  pl.BlockSpec((None, 1, D), lambda i: (l, 0, 0)),
                  pl.BlockSpec((None, D, dff), lambda i: (idx, 0, 0)),
                  pl.BlockSpec((None, D, dff), lambda i: (idx, 0, 0)),
                  pl.BlockSpec((None, dff, D), lambda i: (idx, 0, 0))],
        out_specs=row,
        compiler_params=_cparams(("parallel",)),
        name="ffn",
    )(x, mod4, mod4, mod4, norm_g3, wg, wu, wd)


def _top2(lg, axis, n):
    idx = lax.broadcasted_iota(jnp.int32, lg.shape, axis)
    m1 = jnp.max(lg, axis=axis, keepdims=True)
    i1 = jnp.min(jnp.where(lg == m1, idx, n), axis=axis, keepdims=True)
    rest = jnp.where(idx == i1, NEG_INF, lg)
    m2 = jnp.max(rest, axis=axis, keepdims=True)
    i2 = jnp.min(jnp.where(rest == m2, idx, n), axis=axis, keepdims=True)
    return m1, i1, m2, i2


def _pack_halves(hb):
    lo = pltpu.bitcast(hb[:, :D // 2].astype(F32), jnp.uint32)
    hi = pltpu.bitcast(hb[:, D // 2:].astype(F32), jnp.uint32)
    return (lo >> 16) | hi


def _unpack_halves(u):
    lo = pltpu.bitcast(u << 16, F32).astype(BF16)
    hi = pltpu.bitcast(u & jnp.uint32(0xFFFF0000), F32).astype(BF16)
    return lo, hi


def _route_kernel(x_ref, shift_ref, scale_ref, g_ref, wr_ref, hp_ref, wts_ref, meta_ref, cnt_ref, run_s):
    @pl.when(pl.program_id(0) == 0)
    def _():
        run_s[...] = jnp.zeros_like(run_s)

    h = _norm_mod(x_ref[...], g_ref, shift_ref, scale_ref)
    hp_ref[...] = _pack_halves(h.astype(BF16))
    logits = jnp.dot(h, wr_ref[...], preferred_element_type=F32, precision=lax.Precision.HIGHEST)
    lane = lax.broadcasted_iota(jnp.int32, logits.shape, 1)
    m1, _, m2, _ = _top2(jnp.where(lane < N_EXPERTS, logits, NEG_INF), 1, LANES)
    e2 = jnp.exp(m2 - m1)
    den = 1.0 + e2
    wts_ref[...] = jnp.where(lane == 0, 1.0 / den, jnp.where(lane == 1, e2 / den, 0.0))

    lt = logits.T[:N_EXPERTS, :]
    _, i1, _, i2 = _top2(lt, 0, N_EXPERTS)
    sub = lax.broadcasted_iota(jnp.int32, lt.shape, 0)
    onehot = jnp.where((sub == i1) | (sub == i2), 1.0, 0.0)
    src = lax.broadcasted_iota(jnp.int32, (TM, TM), 0)
    dst = lax.broadcasted_iota(jnp.int32, (TM, TM), 1)
    earlier = jnp.where(src < dst, 1.0, 0.0).astype(BF16)
    before = _dot(onehot.astype(BF16), earlier) + run_s[:, 0:1]
    rank1 = jnp.sum(jnp.where(sub == i1, before, 0.0), axis=0, keepdims=True).astype(jnp.int32)
    rank2 = jnp.sum(jnp.where(sub == i2, before, 0.0), axis=0, keepdims=True).astype(jnp.int32)
    meta_ref[...] = jnp.where(sub == 0, i1, jnp.where(sub == 1, i2, jnp.where(sub == 2, rank1,
                              jnp.where(sub == 3, rank2, 0))))
    run_s[...] = run_s[...] + jnp.sum(onehot, axis=1, keepdims=True)
    cnt_ref[...] = run_s[...]


def _route(x, mod4, norm_g3, wr, l, idx, row_of_tile):
    t = x.shape[0]
    nt = t // TM
    return pl.pallas_call(
        _route_kernel,
        out_shape=[jax.ShapeDtypeStruct((t, D // 2), jnp.uint32), jax.ShapeDtypeStruct((t, LANES), F32),
                   jax.ShapeDtypeStruct((nt, N_EXPERTS, TM), jnp.int32),
                   jax.ShapeDtypeStruct((N_EXPERTS, LANES), F32)],
        grid=(nt,),
        in_specs=[pl.BlockSpec((TM, D), lambda i: (i, 0)),
                  _mod_spec(l, 3, row_of_tile), _mod_spec(l, 4, row_of_tile),
                  pl.BlockSpec((None, 1, D), lambda i: (l, 0, 0)),
                  pl.BlockSpec((None, D, LANES), lambda i: (idx, 0, 0))],
        out_specs=[pl.BlockSpec((TM, D // 2), lambda i: (i, 0)), pl.BlockSpec((TM, LANES), lambda i: (i, 0)),
                   pl.BlockSpec((None, N_EXPERTS, TM), lambda i: (i, 0, 0)),
                   pl.BlockSpec((N_EXPERTS, LANES), lambda i: (0, 0))],
        scratch_shapes=[pltpu.VMEM((N_EXPERTS, LANES), F32)],
        compiler_params=_cparams(("arbitrary",)),
        name="moe_route",
    )(x, mod4, mod4, norm_g3, wr)


def _row_copy(src, src_row, dst, dst_row, sem):
    return pltpu.make_async_copy(src.at[pl.ds(src_row, 1)], dst.at[pl.ds(dst_row, 1)], sem)


ROWS_PER_ISSUE = 8


def _for_each_assignment(fn):
    def trip(g, carry):
        row0 = pl.multiple_of(g * ROWS_PER_ISSUE, ROWS_PER_ISSUE)
        for j in range(ROWS_PER_ISSUE):
            for k in range(2):
                fn(row0 + j, k, 2 * row0 + (2 * j + k))
        return carry

    lax.fori_loop(0, TM // ROWS_PER_ISSUE, trip, 0)


def _scatter_kernel(dest_ref, tail_ref, hp_ref, xs_ref, zero_s, sem):
    _for_each_assignment(lambda r, k, f: _row_copy(hp_ref, r, xs_ref, dest_ref[0, f], sem).start())
    _for_each_assignment(lambda r, k, f: _row_copy(hp_ref, 0, xs_ref, 0, sem).wait())

    @pl.when(pl.program_id(0) == pl.num_programs(0) - 1)
    def _():
        zero_s[...] = jnp.zeros_like(zero_s)
        for e in range(N_EXPERTS):
            first, count = tail_ref[0, e], tail_ref[1, e]

            def start(r, carry):
                _row_copy(zero_s, 0, xs_ref, first + r, sem).start()
                return carry

            def wait(r, carry):
                _row_copy(zero_s, 0, xs_ref, 0, sem).wait()
                return carry

            lax.fori_loop(0, count, start, 0)
            lax.fori_loop(0, count, wait, 0)

        def block_copy(b):
            return pltpu.make_async_copy(zero_s, xs_ref.at[pl.ds(pl.multiple_of(b * BM, BM), BM)], sem)

        def start_block(j, carry):
            block_copy(tail_ref[0, N_EXPERTS] + j).start()
            return carry

        def wait_block(j, carry):
            block_copy(0).wait()
            return carry

        lax.fori_loop(0, tail_ref[1, N_EXPERTS], start_block, 0)
        lax.fori_loop(0, tail_ref[1, N_EXPERTS], wait_block, 0)


def _scatter(dest, tails, hp, rows):
    nt = dest.shape[0]
    return pl.pallas_call(
        _scatter_kernel,
        out_shape=jax.ShapeDtypeStruct((rows, D // 2), jnp.uint32),
        grid=(nt,),
        in_specs=[pl.BlockSpec((None, 1, 2 * TM), lambda i: (i, 0, 0), memory_space=pltpu.SMEM),
                  pl.BlockSpec(memory_space=pltpu.SMEM),
                  pl.BlockSpec((TM, D // 2), lambda i: (i, 0))],
        out_specs=pl.BlockSpec(memory_space=pl.ANY),
        scratch_shapes=[pltpu.VMEM((BM, D // 2), jnp.uint32), pltpu.SemaphoreType.DMA(())],
        compiler_params=_cparams(("arbitrary",)),
        name="moe_scatter",
    )(dest, tails, hp)


def _experts_kernel(blk_in_ref, blk_e_ref, n_used_ref, xs_ref, wg_ref, wu_ref, wd_ref, ys_ref):
    del blk_in_ref, blk_e_ref
    used = pl.program_id(0) < n_used_ref[0]

    @pl.when(used)
    def _():
        lo, hi = _unpack_halves(xs_ref[...])
        half = D // 2
        a = _dot(lo, wg_ref[:half, :]) + _dot(hi, wg_ref[half:, :])
        b = _dot(lo, wu_ref[:half, :]) + _dot(hi, wu_ref[half:, :])
        ys_ref[...] = _dot((a * _sigmoid(a) * b).astype(BF16), wd_ref[...])

    @pl.when(jnp.logical_not(used))
    def _():
        ys_ref[...] = jnp.zeros_like(ys_ref)


def _experts(xs, tables, wg, wu, wd, idx):
    nblk = xs.shape[0] // BM
    dfe = wg.shape[3]
    grid_spec = pltpu.PrefetchScalarGridSpec(
        num_scalar_prefetch=3,
        grid=(nblk,),
        in_specs=[pl.BlockSpec((BM, D // 2), lambda i, bi, be, nu: (bi[i], 0)),
                  pl.BlockSpec((None, None, D, dfe), lambda i, bi, be, nu: (idx, be[i], 0, 0)),
                  pl.BlockSpec((None, None, D, dfe), lambda i, bi, be, nu: (idx, be[i], 0, 0)),
                  pl.BlockSpec((None, None, dfe, D), lambda i, bi, be, nu: (idx, be[i], 0, 0))],
        out_specs=pl.BlockSpec((BM, D), lambda i, bi, be, nu: (i, 0)),
    )
    return pl.pallas_call(
        _experts_kernel,
        out_shape=jax.ShapeDtypeStruct((nblk * BM, D), F32),
        grid_spec=grid_spec,
        compiler_params=_cparams(("arbitrary",)),
        name="moe_experts",
    )(*tables, xs, wg, wu, wd)


def _combine_kernel(dest_ref, x_ref, gate_ref, wts_ref, ys_ref, o_ref, buf, sem):
    _for_each_assignment(lambda r, k, f: _row_copy(ys_ref, dest_ref[0, f], buf.at[k], r, sem).start())
    _for_each_assignment(lambda r, k, f: _row_copy(ys_ref, 0, buf.at[k], 0, sem).wait())
    w = wts_ref[...]
    y = w[:, 0:1] * buf[0] + w[:, 1:2] * buf[1]
    o_ref[...] = x_ref[...] + gate_ref[...] * y


def _combine(dest, x, mod4, wts, ys, l, row_of_tile):
    t = x.shape[0]
    row = pl.BlockSpec((TM, D), lambda i: (i, 0))
    return pl.pallas_call(
        _combine_kernel,
        out_shape=jax.ShapeDtypeStruct((t, D), F32),
        grid=(t // TM,),
        in_specs=[pl.BlockSpec((None, 1, 2 * TM), lambda i: (i, 0, 0), memory_space=pltpu.SMEM),
                  row, _mod_spec(l, 5, row_of_tile),
                  pl.BlockSpec((TM, LANES), lambda i: (i, 0)),
                  pl.BlockSpec(memory_space=pl.ANY)],
        out_specs=row,
        scratch_shapes=[pltpu.VMEM((2, TM, D), F32), pltpu.SemaphoreType.DMA(())],
        compiler_params=_cparams(("arbitrary",)),
        name="moe_combine",
    )(dest, x, mod4, wts, ys)


def _moe(x, mod4, norm_g3, wr, wg, wu, wd, l, idx, row_of_tile):
    t = x.shape[0]
    nblk = 2 * t // BM + N_EXPERTS
    hp, wts, meta, cnt = _route(x, mod4, norm_g3, wr, l, idx, row_of_tile)
    counts = cnt[:, 0].astype(jnp.int32)
    blocks = (counts + BM - 1) // BM
    ends = jnp.cumsum(blocks)
    n_used = ends[-1]
    base = (ends - blocks) * BM
    experts = jnp.arange(N_EXPERTS, dtype=jnp.int32)

    def base_of(e):
        return jnp.sum(jnp.where(e[..., None] == experts, base, 0), axis=-1)

    dest = jnp.stack([base_of(meta[:, 0]) + meta[:, 2], base_of(meta[:, 1]) + meta[:, 3]], axis=-1)
    dest = dest.reshape(dest.shape[0], 1, 2 * TM)
    step = jnp.arange(nblk, dtype=jnp.int32)
    blk_in = jnp.minimum(step, jnp.maximum(n_used - 1, 0))
    blk_e = jnp.minimum(jnp.sum(blk_in[:, None] >= ends[None, :], axis=1), N_EXPERTS - 1).astype(jnp.int32)
    tables = (blk_in, blk_e, n_used.reshape(1))
    tails = jnp.stack([jnp.append(base + counts, n_used), jnp.append(blocks * BM - counts, nblk - n_used)])
    xs = _scatter(dest, tails, hp, nblk * BM)
    ys = _experts(xs, tables, wg, wu, wd, idx)
    return _combine(dest, x, mod4, wts, ys, l, row_of_tile)


def _final_kernel(x_ref, g_ref, o_ref):
    o_ref[...] = _rms(x_ref[...], g_ref[...])


def _final_norm(x, g2):
    t = x.shape[0]
    return pl.pallas_call(
        _final_kernel,
        out_shape=jax.ShapeDtypeStruct((t, D), F32),
        grid=(t // TM,),
        in_specs=[pl.BlockSpec((TM, D), lambda i: (i, 0)), pl.BlockSpec((1, D), lambda i: (0, 0))],
        out_specs=pl.BlockSpec((TM, D), lambda i: (i, 0)),
        compiler_params=_cparams(("parallel",)),
        name="final_norm",
    )(x, g2)


def _relayout_w_in(w_in):
    depth = w_in.shape[0]
    z = lambda n: jnp.zeros((depth, D, n), w_in.dtype)
    aq, ak, av = w_in[..., 0:512], w_in[..., 512:1024], w_in[..., 1024:1536]
    bq, bk, bv = w_in[..., 1536:2048], w_in[..., 2048:2176], w_in[..., 2176:2304]
    cq, ckv, ckr = w_in[..., 2304:2560], w_in[..., 2560:2688], w_in[..., 2688:2720]
    gates = w_in[..., 2720:2720 + 3 * D]
    pieces = [aq, ak, av]
    for h in range(B_HEADS):
        qh = bq[..., h * HD:(h + 1) * HD]
        pieces += [qh, z(HD)] if h // B_GROUP == 0 else [z(HD), qh]
    pieces += [bk, bv, cq, ckv, z(KR_OFF), ckr, z(LANES - KR_OFF - C_ROPE)]
    w_a = jnp.concatenate(pieces, axis=-1).astype(BF16)
    assert w_a.shape[-1] == NPA
    return w_a, gates.astype(BF16)


def _relayout_mla(w_q_up, w_kv_up):
    depth = w_q_up.shape[0]
    qd = C_NOPE + C_ROPE
    wq = jnp.pad(w_q_up.reshape(depth, C_Q_LORA, C_HEADS, qd), ((0, 0), (0, 0), (0, 0), (0, LANES - qd)))
    kv = w_kv_up.reshape(depth, C_KV_LORA, C_HEADS, C_NOPE + C_VDIM)
    wkn = jnp.pad(kv[..., :C_NOPE], ((0, 0), (0, 0), (0, 0), (0, LANES - C_NOPE)))
    wv = jnp.pad(kv[..., C_NOPE:], ((0, 0), (0, 0), (0, 0), (0, LANES - C_VDIM)))
    flat = lambda w: w.reshape(depth, w.shape[1], C_HEADS * LANES).astype(BF16)
    return flat(wq), flat(wkn), flat(wv)


def _relayout_w_branch(w_branch):
    depth = w_branch.shape[0]
    wba = w_branch[:, 0]
    b = w_branch[:, 1].reshape(depth, B_HEADS, HD, D)
    zb = jnp.zeros_like(b)
    lo = jnp.concatenate([b, zb], axis=2)
    hi = jnp.concatenate([zb, b], axis=2)
    grp = (jnp.arange(B_HEADS) // B_GROUP).reshape(1, B_HEADS, 1, 1)
    wbb = jnp.where(grp == 0, lo, hi).reshape(depth, B_HEADS * LANES, D)
    c = w_branch[:, 2].reshape(depth, C_HEADS, C_VDIM, D)
    wbc = jnp.pad(c, ((0, 0), (0, 0), (0, LANES - C_VDIM), (0, 0))).reshape(depth, C_HEADS * LANES, D)
    return wba.astype(BF16), wbb.astype(BF16), wbc.astype(BF16)


def _rope_tables(n_tokens, rot_dim, lane_off, group):
    rows_n = n_tokens // GRID_W
    row, col = jnp.meshgrid(jnp.arange(rows_n), jnp.arange(GRID_W), indexing="ij")
    row = row.reshape(-1).astype(F32)
    col = col.reshape(-1).astype(F32)
    n_freq = rot_dim // 4
    inv = jnp.power(ROPE_THETA, -jnp.arange(n_freq, dtype=F32) / n_freq)
    ang = jnp.concatenate([row[:, None] * inv, col[:, None] * inv], axis=-1)
    cos, sin = jnp.cos(ang), jnp.sin(ang)
    half = rot_dim // 2
    c = jnp.ones((n_tokens, LANES), F32)
    sa = jnp.zeros((n_tokens, LANES), F32)
    sb = jnp.zeros((n_tokens, LANES), F32)
    for off in range(lane_off, LANES - rot_dim + 1, group):
        c = c.at[:, off:off + half].set(cos).at[:, off + half:off + rot_dim].set(cos)
        sa = sa.at[:, off:off + half].set(-sin)
        sb = sb.at[:, off + half:off + rot_dim].set(sin)
        if group >= LANES:
            break
    ident = (jnp.ones((TM, LANES), F32), jnp.zeros((TM, LANES), F32), jnp.zeros((TM, LANES), F32))
    return tuple(jnp.concatenate([a, i], axis=0) for a, i in zip((c, sa, sb), ident))


def _run_group(x, nb, s, row_of_tile, tab_block, tabs, caches, mod4, w):
    depth = w["w_a"].shape[0]
    t = nb * s
    assert t % TM == 0 and (s % TM == 0 or TM % s == 0)
    pas, ckvns = [], []
    for l in range(depth):
        qa, ka, va, qb, kb, vb, qc, kc, vc, ckvn, *raw = _proj(
            x, mod4, w["norm_attn_g"], w["w_a"], tabs, w["gq"], w["gkv"], w["wq"], w["wkn"], w["wv"],
            l, row_of_tile, tab_block, caches is None)
        if caches is None:
            ctx_a = ctx_b = ctx_c = None
            dk, dv, kv_b, kr = raw
            pas.append((dk, dv, kv_b[:, :LANES], kv_b[:, LANES:], kr[:, KR_OFF:KR_OFF + C_ROPE]))
            ckvns.append(ckvn)
        else:
            ctx_a, ctx_b, ctx_c = caches
        oa = _attn_a(qa, ka, va, ctx_a, w["lam4"], w["subg"], l, nb, s)
        ob = _attn_b(qb, kb, vb, ctx_b, w["win_sink"], l, nb, s)
        oc = _attn_c(qc, kc, vc, ctx_c, w["wkn"], w["wv"], l, nb, s)
        x = _merge(x, mod4, w["norm_attn_g"], w["w_gates"], oa, ob, oc,
                   w["wba"], w["wbb"], w["wbc"], w["w_out"], l, row_of_tile)
        if l % 2 == 0:
            x = _ffn(x, mod4, w["norm_ffn_g"], w["ffn_g"], w["ffn_u"], w["ffn_d"], l, l // 2, row_of_tile)
        else:
            x = _moe(x, mod4, w["norm_ffn_g"], w["moe_r"], w["moe_g"], w["moe_u"], w["moe_d"],
                     l, l // 2, row_of_tile)
    return _final_norm(x, w["final_g"]), pas, ckvns


def kernel(x_prompt, x_sample, cache_diff_k, cache_diff_v, cache_win_k, cache_win_v, cache_mla_ckv, cache_mla_krope, c, c_ctx, w_ada, b_ada, norm_attn_g, norm_ffn_g, w_in, diff_lambda_q1, diff_lambda_k1, diff_lambda_q2, diff_lambda_k2, diff_subln_g, win_sink, mla_q_norm_g, mla_w_q_up, mla_kv_norm_g, mla_w_kv_up, w_branch, w_out, ffn_w_gate, ffn_w_up, ffn_w_down, moe_w_router, moe_w_gate, moe_w_up, moe_w_down, final_norm_g):
    depth = w_in.shape[0]
    nbc, sc, _ = x_prompt.shape
    nbl, sl, _ = x_sample.shape
    past = cache_diff_k.shape[2]
    assert 1 + nbl <= MOD_ROWS

    cond = jnp.zeros((MOD_ROWS, D), F32).at[0].set(c_ctx).at[1:1 + nbl].set(c)
    mod4 = _ada_mod(cond, w_ada, b_ada).reshape(depth, MOD_ROWS, 1, 6 * D)

    w_a, w_gates = _relayout_w_in(w_in)
    wq, wkn, wv = _relayout_mla(mla_w_q_up, mla_w_kv_up)
    wba, wbb, wbc = _relayout_w_branch(w_branch)
    vec3 = lambda a: a.reshape(a.shape[0], 1, a.shape[1])
    w = dict(
        w_a=w_a, w_gates=w_gates, wq=wq, wkn=wkn, wv=wv, wba=wba, wbb=wbb, wbc=wbc,
        w_out=w_out.astype(BF16), norm_attn_g=vec3(norm_attn_g), norm_ffn_g=vec3(norm_ffn_g),
        gq=vec3(mla_q_norm_g), gkv=vec3(mla_kv_norm_g), subg=vec3(diff_subln_g),
        lam4=[vec3(a) for a in (diff_lambda_q1, diff_lambda_k1, diff_lambda_q2, diff_lambda_k2)],
        win_sink=win_sink,
        ffn_g=ffn_w_gate.astype(BF16), ffn_u=ffn_w_up.astype(BF16), ffn_d=ffn_w_down.astype(BF16),
        moe_r=jnp.pad(moe_w_router, ((0, 0), (0, 0), (0, LANES - N_EXPERTS))),
        moe_g=moe_w_gate.astype(BF16), moe_u=moe_w_up.astype(BF16), moe_d=moe_w_down.astype(BF16),
        final_g=final_norm_g.reshape(1, D),
    )

    tabs_h = _rope_tables(sl, HD, 0, HD)
    tabs_r = _rope_tables(sl, C_ROPE, KR_OFF, LANES)
    tabs = tabs_h + tabs_r
    ident_block = sl // TM

    y_c, pas, ckvns = _run_group(
        x_prompt.reshape(nbc * sc, D), nbc, sc, lambda i: 0, lambda i: ident_block, tabs, None, mod4, w)

    tiles_per_b = sl // TM
    krp = jnp.pad(cache_mla_krope, ((0, 0), (0, 0), (0, 0), (KR_OFF, LANES - KR_OFF - C_ROPE)))
    caches = (
        (cache_diff_k.reshape(nbl, depth, past, A_HEADS * LANES),
         cache_diff_v.reshape(nbl, depth, past, A_HEADS * LANES)),
        (cache_win_k.reshape(nbl, depth, past, LANES), cache_win_v.reshape(nbl, depth, past, LANES)),
        (cache_mla_ckv, krp),
    )
    y_l, _, _ = _run_group(
        x_sample.reshape(nbl * sl, D), nbl, sl, lambda i: 1 + i // tiles_per_b,
        lambda i: i % tiles_per_b, tabs, caches, mod4, w)

    def cache(per_layer, tail):
        return jnp.stack([a.reshape((nbc, sc) + tail) for a in per_layer], axis=1)

    new_diff_k = cache([p[0] for p in pas], (A_HEADS, 2 * HD))
    new_diff_v = cache([p[1] for p in pas], (A_HEADS, 2 * HD))
    new_win_k = cache([p[2] for p in pas], (B_HEADS // B_GROUP, HD))
    new_win_v = cache([p[3] for p in pas], (B_HEADS // B_GROUP, HD))
    new_mla_ckv = cache(ckvns, (C_KV_LORA,))
    new_mla_krope = cache([p[4] for p in pas], (C_ROPE,))
    return (y_c.reshape(nbc, sc, D), y_l.reshape(nbl, sl, D), new_diff_k, new_diff_v,
            new_win_k, new_win_v, new_mla_ckv, new_mla_krope)
```

```python
import functools
import math

import jax
import jax.numpy as jnp
from jax import lax
from jax.experimental import pallas as pl
from jax.experimental.pallas import tpu as pltpu

F32 = jnp.float32
BF16 = jnp.bfloat16

D = 1024
HD = 64
LANES = 128
GRID_W = 64
BLOCK = 128
WINDOW = 128
A_HEADS = 4
B_HEADS = 8
B_GROUP = 4
C_HEADS = 8
C_Q_LORA = 256
C_KV_LORA = 128
C_NOPE = 64
C_ROPE = 32
C_VDIM = 64
N_EXPERTS = 8
ROPE_THETA = 10000.0
EPS = 1e-6
NEG_INF = -1e30
LOG2E = math.log2(math.e)
MOD_ROWS = 16

TM = 512
TQ = 256
TQC = 512
TQB = 512
BM = 512
VMEM_LIMIT = 56 * 1024 * 1024

CH_AQ, CH_AK, CH_AV, CH_BQ, CH_BK, CH_BV, CH_CQ, CH_CKV, CH_CKR = 0, 4, 8, 12, 20, 21, 22, 24, 25
N_CH = 26
NPA = N_CH * LANES
KR_OFF = C_NOPE


def _cparams(sem):
    return pltpu.CompilerParams(dimension_semantics=sem, vmem_limit_bytes=VMEM_LIMIT)


def _rms(x, g):
    return x * lax.rsqrt(jnp.mean(x * x, axis=-1, keepdims=True) + EPS) * g


def _sigmoid(x):
    return 1.0 / (1.0 + jnp.exp(-x))


def _dot(a, b):
    return jnp.dot(a, b, preferred_element_type=F32)


def _dot_t(a, b):
    return lax.dot_general(a, b, (((1,), (1,)), ((), ())), preferred_element_type=F32)


def _ada_kernel(c_ref, w_ref, b_ref, o_ref):
    c = c_ref[...]
    s = (c * _sigmoid(c)).astype(BF16)
    o_ref[...] = _dot(s, w_ref[...].astype(BF16)) + b_ref[...]


def _ada_mod(cond, w_ada, b_ada):
    depth, _, n = w_ada.shape
    tn = 1536
    return pl.pallas_call(
        _ada_kernel,
        out_shape=jax.ShapeDtypeStruct((depth, MOD_ROWS, n), F32),
        grid=(depth, n // tn),
        in_specs=[
            pl.BlockSpec((MOD_ROWS, D), lambda l, j: (0, 0)),
            pl.BlockSpec((None, D, tn), lambda l, j: (l, 0, j)),
            pl.BlockSpec((None, 1, tn), lambda l, j: (l, 0, j)),
        ],
        out_specs=pl.BlockSpec((None, MOD_ROWS, tn), lambda l, j: (l, 0, j)),
        compiler_params=_cparams(("parallel", "parallel")),
        name="ada_mod",
    )(cond, w_ada, b_ada.reshape(depth, 1, n))


def _mod_spec(l, k, row_of_tile):
    return pl.BlockSpec((None, None, 1, D), lambda i: (l, row_of_tile(i), 0, k))


def _norm_mod(x, g_ref, shift_ref, scale_ref):
    return _rms(x, g_ref[...]) * (1.0 + scale_ref[...]) + shift_ref[...]


def _rope(x, c, sa, sb, half):
    return x * c + pltpu.roll(x, LANES - half, 1) * sa + pltpu.roll(x, half, 1) * sb


def _proj_kernel(emit_cache, x_ref, shift_ref, scale_ref, g_ref, w_ref,
                 hc_ref, hsa_ref, hsb_ref, rc_ref, rsa_ref, rsb_ref, gq_ref, gkv_ref,
                 wq_ref, wkn_ref, wv_ref,
                 qa_ref, ka_ref, va_ref, qb_ref, kb_ref, vb_ref, qc_ref, kc_ref, vc_ref, ckvn_ref,
                 *cache_refs):
    h = _norm_mod(x_ref[...], g_ref, shift_ref, scale_ref).astype(BF16)
    pa = _dot(h, w_ref[...])
    hc, hsa, hsb = hc_ref[...], hsa_ref[...], hsb_ref[...]
    rc, rsa, rsb = rc_ref[...], rsa_ref[...], rsb_ref[...]

    def chunk(c, n=1):
        return pa[:, c * LANES:(c + n) * LANES]

    def put(ref, c, val):
        ref[:, c * LANES:(c + 1) * LANES] = val.astype(ref.dtype)

    if emit_cache:
        dk_ref, dv_ref, kv_b_ref, kr_ref = cache_refs
        for hd in range(A_HEADS):
            dk_ref[:, hd, :] = chunk(CH_AK + hd)
            dv_ref[:, hd, :] = chunk(CH_AV + hd)
        kv_b_ref[...] = chunk(CH_BK, 2)
        kr_ref[...] = chunk(CH_CKR)

    q_scale = HD ** -0.5
    for c in range(A_HEADS):
        put(qa_ref, c, _rope(chunk(CH_AQ + c), hc, hsa, hsb, HD // 2) * (q_scale * LOG2E))
        put(ka_ref, c, _rope(chunk(CH_AK + c), hc, hsa, hsb, HD // 2))
        put(va_ref, c, chunk(CH_AV + c))
    for c in range(B_HEADS):
        put(qb_ref, c, _rope(chunk(CH_BQ + c), hc, hsa, hsb, HD // 2) * (q_scale * LOG2E))
    put(kb_ref, 0, _rope(chunk(CH_BK), hc, hsa, hsb, HD // 2))
    put(vb_ref, 0, chunk(CH_BV))

    cqn = _rms(chunk(CH_CQ, 2), gq_ref[...]).astype(BF16)
    qup = _dot(cqn, wq_ref[...])
    c_scale = (C_NOPE + C_ROPE) ** -0.5 * LOG2E
    for c in range(C_HEADS):
        put(qc_ref, c, _rope(qup[:, c * LANES:(c + 1) * LANES], rc, rsa, rsb, C_ROPE // 2) * c_scale)

    ckvn = _rms(chunk(CH_CKV), gkv_ref[...])
    ckvn_ref[...] = ckvn
    ckvb = ckvn.astype(BF16)
    kn = _dot(ckvb, wkn_ref[...])
    kr = _rope(chunk(CH_CKR), rc, rsa, rsb, C_ROPE // 2)
    for c in range(C_HEADS):
        put(kc_ref, c, kn[:, c * LANES:(c + 1) * LANES] + kr)
    vc_ref[...] = _with_ones_lane(_dot(ckvb, wv_ref[...])).astype(BF16)


def _proj(x, mod4, norm_g3, w_a, tabs, gq3, gkv3, wq, wkn, wv, l, row_of_tile, tab_block, emit_cache):
    t = x.shape[0]
    tab_spec = pl.BlockSpec((TM, LANES), lambda i: (tab_block(i), 0))
    def wide(n):
        if isinstance(n, tuple):
            return pl.BlockSpec((TM,) + n, lambda i: (i, 0, 0))
        return pl.BlockSpec((TM, n), lambda i: (i, 0))

    outs = [(4 * LANES, BF16)] * 3 + [(8 * LANES, BF16), (LANES, BF16), (LANES, BF16)] \
        + [(8 * LANES, BF16)] * 3 + [(LANES, F32)]
    if emit_cache:
        outs += [((A_HEADS, LANES), F32)] * 2 + [(2 * LANES, F32), (LANES, F32)]
    shape = lambda n: (t,) + n if isinstance(n, tuple) else (t, n)
    return pl.pallas_call(
        functools.partial(_proj_kernel, emit_cache),
        out_shape=[jax.ShapeDtypeStruct(shape(n), dt) for n, dt in outs],
        grid=(t // TM,),
        in_specs=[
            wide(D), _mod_spec(l, 0, row_of_tile), _mod_spec(l, 1, row_of_tile),
            pl.BlockSpec((None, 1, D), lambda i: (l, 0, 0)),
            pl.BlockSpec((None, D, NPA), lambda i: (l, 0, 0)),
        ] + [tab_spec] * 6 + [
            pl.BlockSpec((None, 1, C_Q_LORA), lambda i: (l, 0, 0)),
            pl.BlockSpec((None, 1, C_KV_LORA), lambda i: (l, 0, 0)),
            pl.BlockSpec((None, C_Q_LORA, C_HEADS * LANES), lambda i: (l, 0, 0)),
            pl.BlockSpec((None, C_KV_LORA, C_HEADS * LANES), lambda i: (l, 0, 0)),
            pl.BlockSpec((None, C_KV_LORA, C_HEADS * LANES), lambda i: (l, 0, 0)),
        ],
        out_specs=[wide(n) for n, _ in outs],
        compiler_params=_cparams(("parallel",)),
        name="proj",
    )(x, mod4, mod4, norm_g3, w_a, *tabs, gq3, gkv3, wq, wkn, wv)


def _exp2_parts(scores):
    m = functools.reduce(jnp.maximum, [jnp.max(s, axis=-1, keepdims=True) for s in scores])
    es = [jnp.exp2(s - m) for s in scores]
    den = functools.reduce(lambda a, b: a + b, [jnp.sum(e, axis=-1, keepdims=True) for e in es])
    return es, den


def _attn_a_kernel(lam_init, has_ctx, q_ref, k_ref, v_ref, *rest):
    if has_ctx:
        kc_ref, vc_ref, lq1_ref, lk1_ref, lq2_ref, lk2_ref, sg_ref, o_ref = rest
    else:
        lq1_ref, lk1_ref, lq2_ref, lk2_ref, sg_ref, o_ref = rest
    lam = (jnp.exp(jnp.sum(lq1_ref[...] * lk1_ref[...], axis=-1, keepdims=True))
           - jnp.exp(jnp.sum(lq2_ref[...] * lk2_ref[...], axis=-1, keepdims=True)) + lam_init)
    lane = lax.broadcasted_iota(jnp.int32, (q_ref.shape[0], LANES), 1)
    for h in range(A_HEADS):
        sl = slice(h * LANES, (h + 1) * LANES)
        q = q_ref[:, sl]
        zero = jnp.zeros_like(q)
        q1 = jnp.where(lane < HD, q, zero)
        q2 = jnp.where(lane >= HD, q, zero)
        keys, vals = [k_ref[:, sl]], [v_ref[:, sl]]
        if has_ctx:
            keys.append(kc_ref[:, sl].astype(BF16))
            vals.append(vc_ref[:, sl].astype(BF16))
        e1, l1 = _exp2_parts([_dot_t(q1, k) for k in keys])
        e2, l2 = _exp2_parts([_dot_t(q2, k) for k in keys])
        c = lam * l1 / l2
        o = None
        for a1, a2, v in zip(e1, e2, vals):
            part = _dot((a1 - c * a2).astype(BF16), v)
            o = part if o is None else o + part
        o = o * (1.0 / l1)
        o_ref[:, sl] = (_rms(o, sg_ref[...]) * (1.0 - lam_init)).astype(o_ref.dtype)


def _attn_a(qa, ka, va, ctx, lam4, subg3, l, nb, s):
    t = qa.shape[0]
    tq = min(TQ, s)
    nq = s // tq
    w = A_HEADS * LANES
    lam_init = 0.8 - 0.6 * math.exp(-0.3 * l)
    in_specs = [
        pl.BlockSpec((tq, w), lambda b, i: (b * nq + i, 0)),
        pl.BlockSpec((s, w), lambda b, i: (b, 0)),
        pl.BlockSpec((s, w), lambda b, i: (b, 0)),
    ]
    args = [qa, ka, va]
    if ctx is not None:
        past = ctx[0].shape[2]
        in_specs += [pl.BlockSpec((None, None, past, w), lambda b, i: (b, l, 0, 0))] * 2
        args += list(ctx)
    in_specs += [pl.BlockSpec((None, 1, HD), lambda b, i: (l, 0, 0))] * 4
    in_specs += [pl.BlockSpec((None, 1, LANES), lambda b, i: (l, 0, 0))]
    return pl.pallas_call(
        functools.partial(_attn_a_kernel, lam_init, ctx is not None),
        out_shape=jax.ShapeDtypeStruct((t, w), BF16),
        grid=(nb, nq),
        in_specs=in_specs,
        out_specs=pl.BlockSpec((tq, w), lambda b, i: (b * nq + i, 0)),
        compiler_params=_cparams(("parallel", "arbitrary")),
        name="attn_a_lat" if ctx is not None else "attn_a_ctx",
    )(*args, *lam4, subg3)


def _sink_attend(qs, keys, vals, biases, sinkv):
    scores = []
    for k, bias in zip(keys, biases):
        s = _dot_t(qs, k)
        scores.append(s if bias is None else s + bias)
    m = functools.reduce(jnp.maximum, [jnp.max(s, axis=-1, keepdims=True) for s in scores] + [sinkv])
    es = [jnp.exp2(s - m) for s in scores]
    den = functools.reduce(lambda a, b: a + b,
                           [jnp.sum(e, axis=-1, keepdims=True) for e in es] + [jnp.exp2(sinkv - m)])
    o = None
    for e, v in zip(es, vals):
        part = _dot(e.astype(BF16), v)
        o = part if o is None else o + part
    return o * (1.0 / den)


def _stack_group(q_ref, row0, rows, g):
    return jnp.concatenate(
        [q_ref[pl.ds(row0, rows), (B_GROUP * g + j) * LANES:(B_GROUP * g + j + 1) * LANES]
         for j in range(B_GROUP)], axis=0)


def _sink_column(sink_ref, l, g, rows):
    row = lax.broadcasted_iota(jnp.int32, (B_GROUP * rows, 1), 0)
    col = jnp.full((B_GROUP * rows, 1), sink_ref[l, B_GROUP * g + B_GROUP - 1], F32)
    for j in range(B_GROUP - 2, -1, -1):
        col = jnp.where(row < (j + 1) * rows, sink_ref[l, B_GROUP * g + j], col)
    return col * LOG2E


def _store_group(o_ref, row0, rows, g, o):
    lane = lax.broadcasted_iota(jnp.int32, o.shape, 1)
    o = jnp.where((lane >= g * HD) & (lane < (g + 1) * HD), o, 0.0).astype(o_ref.dtype)
    for j in range(B_GROUP):
        c = B_GROUP * g + j
        o_ref[pl.ds(row0, rows), c * LANES:(c + 1) * LANES] = o[j * rows:(j + 1) * rows, :]


def _attn_b_ctx_kernel(l, sink_ref, q_ref, k_ref, v_ref, o_ref):
    rows = q_ref.shape[0]
    k, v = k_ref[...], v_ref[...]
    for g in range(B_HEADS // B_GROUP):
        qs = _stack_group(q_ref, 0, rows, g)
        o = _sink_attend(qs, [k], [v], [None], _sink_column(sink_ref, l, g, rows))
        _store_group(o_ref, 0, rows, g, o)


def _window_bias():
    q = jnp.arange(B_GROUP * BLOCK)[:, None] % BLOCK
    k = jnp.arange(3 * BLOCK)[None, :]
    d = jnp.arange(3)[:, None, None] * BLOCK
    return jnp.where(jnp.abs(d + q - k) <= WINDOW, 0.0, NEG_INF).astype(F32)


def _attn_b_lat_kernel(l, s_len, sink_ref, q_ref, k_ref, v_ref, kc_ref, vc_ref, bias_ref, o_ref):
    qi = pl.program_id(1)
    kc = kc_ref[...].astype(BF16)
    vc = vc_ref[...].astype(BF16)
    win = 3 * BLOCK
    for n in range(TQB // BLOCK):
        blk = qi * (TQB // BLOCK) + n
        first = jnp.clip(blk - 1, 0, s_len // BLOCK - 3)
        ws = pl.multiple_of(first * BLOCK, BLOCK)
        kw = k_ref[pl.ds(ws, win), :]
        vw = v_ref[pl.ds(ws, win), :]
        bias = bias_ref[blk - first]
        for g in range(B_HEADS // B_GROUP):
            qs = _stack_group(q_ref, n * BLOCK, BLOCK, g)
            o = _sink_attend(qs, [kw, kc], [vw, vc], [bias, None], _sink_column(sink_ref, l, g, BLOCK))
            _store_group(o_ref, n * BLOCK, BLOCK, g, o)


def _attn_b(qb, kb, vb, ctx, win_sink, l, nb, s):
    t = qb.shape[0]
    smem = pl.BlockSpec(memory_space=pltpu.SMEM)
    if ctx is None:
        return pl.pallas_call(
            functools.partial(_attn_b_ctx_kernel, l),
            out_shape=jax.ShapeDtypeStruct((t, B_HEADS * LANES), BF16),
            grid=(nb,),
            in_specs=[smem,
                      pl.BlockSpec((s, B_HEADS * LANES), lambda b: (b, 0)),
                      pl.BlockSpec((s, LANES), lambda b: (b, 0)),
                      pl.BlockSpec((s, LANES), lambda b: (b, 0))],
            out_specs=pl.BlockSpec((s, B_HEADS * LANES), lambda b: (b, 0)),
            compiler_params=_cparams(("parallel",)),
            name="attn_b_ctx",
        )(win_sink, qb, kb, vb)
    assert s % TQB == 0 and s >= 3 * BLOCK
    nq = s // TQB
    past = ctx[0].shape[2]
    return pl.pallas_call(
        functools.partial(_attn_b_lat_kernel, l, s),
        out_shape=jax.ShapeDtypeStruct((t, B_HEADS * LANES), BF16),
        grid=(nb, nq),
        in_specs=[smem,
                  pl.BlockSpec((TQB, B_HEADS * LANES), lambda b, i: (b * nq + i, 0)),
                  pl.BlockSpec((s, LANES), lambda b, i: (b, 0)),
                  pl.BlockSpec((s, LANES), lambda b, i: (b, 0)),
                  pl.BlockSpec((None, None, past, LANES), lambda b, i: (b, l, 0, 0)),
                  pl.BlockSpec((None, None, past, LANES), lambda b, i: (b, l, 0, 0)),
                  pl.BlockSpec((3, B_GROUP * BLOCK, 3 * BLOCK), lambda b, i: (0, 0, 0))],
        out_specs=pl.BlockSpec((TQB, B_HEADS * LANES), lambda b, i: (b * nq + i, 0)),
        compiler_params=_cparams(("parallel", "arbitrary")),
        name="attn_b_lat",
    )(win_sink, qb, kb, vb, *ctx, _window_bias())


def _with_ones_lane(v):
    lane = lax.broadcasted_iota(jnp.int32, v.shape, 1)
    return jnp.where(lane % LANES == C_VDIM, 1.0, v)


def _attn_c_kernel(has_ctx, q_ref, k_ref, v_ref, *rest):
    if has_ctx:
        ckv_ref, krp_ref, wkn_ref, wv_ref, o_ref, kc_s, vc_s = rest

        @pl.when(pl.program_id(1) == 0)
        def _():
            ckv = ckv_ref[...].astype(BF16)
            kn = _dot(ckv, wkn_ref[...])
            krp = krp_ref[...]
            for h in range(C_HEADS):
                sl = slice(h * LANES, (h + 1) * LANES)
                kc_s[:, sl] = (kn[:, sl] + krp).astype(BF16)
            vc_s[...] = _with_ones_lane(_dot(ckv, wv_ref[...])).astype(BF16)
    else:
        (o_ref,) = rest
    for h in range(C_HEADS):
        sl = slice(h * LANES, (h + 1) * LANES)
        q = q_ref[:, sl]
        keys, vals = [k_ref[:, sl]], [v_ref[:, sl]]
        if has_ctx:
            keys.append(kc_s[:, sl])
            vals.append(vc_s[:, sl])
        scores = [_dot_t(q, k) for k in keys]
        m = functools.reduce(jnp.maximum, [jnp.max(s, axis=-1, keepdims=True) for s in scores])
        o = None
        for s, v in zip(scores, vals):
            part = _dot(jnp.exp2(s - m).astype(BF16), v)
            o = part if o is None else o + part
        o_ref[:, sl] = (o * (1.0 / o[:, C_VDIM:C_VDIM + 1])).astype(o_ref.dtype)


def _attn_c(qc, kc, vc, ctx, wkn, wv, l, nb, s):
    t = qc.shape[0]
    tq = min(TQC, s)
    nq = s // tq
    w = C_HEADS * LANES
    in_specs = [
        pl.BlockSpec((tq, w), lambda b, i: (b * nq + i, 0)),
        pl.BlockSpec((s, w), lambda b, i: (b, 0)),
        pl.BlockSpec((s, w), lambda b, i: (b, 0)),
    ]
    args = [qc, kc, vc]
    scratch = []
    if ctx is not None:
        past = ctx[0].shape[2]
        in_specs += [pl.BlockSpec((None, None, past, LANES), lambda b, i: (b, l, 0, 0))] * 2
        in_specs += [pl.BlockSpec((None, C_KV_LORA, w), lambda b, i: (l, 0, 0))] * 2
        args += list(ctx) + [wkn, wv]
        scratch = [pltpu.VMEM((past, w), BF16)] * 2
    return pl.pallas_call(
        functools.partial(_attn_c_kernel, ctx is not None),
        out_shape=jax.ShapeDtypeStruct((t, w), BF16),
        grid=(nb, nq),
        in_specs=in_specs,
        out_specs=pl.BlockSpec((tq, w), lambda b, i: (b * nq + i, 0)),
        scratch_shapes=scratch,
        compiler_params=_cparams(("parallel", "arbitrary")),
        name="attn_c_lat" if ctx is not None else "attn_c_ctx",
    )(*args)


def _merge_kernel(x_ref, shift_ref, scale_ref, gate_ref, g_ref, wg_ref, oa_ref, ob_ref, oc_ref,
                  wba_ref, wbb_ref, wbc_ref, wo_ref, o_ref):
    x = x_ref[...]
    h = _norm_mod(x, g_ref, shift_ref, scale_ref).astype(BF16)
    mix = None
    for n, (br_ref, wb_ref) in enumerate(((oa_ref, wba_ref), (ob_ref, wbb_ref), (oc_ref, wbc_ref))):
        gate = _sigmoid(_dot(h, wg_ref[:, n * D:(n + 1) * D]))
        term = gate * _dot(br_ref[...], wb_ref[...])
        mix = term if mix is None else mix + term
    o_ref[...] = x + gate_ref[...] * _dot(mix.astype(BF16), wo_ref[...])


def _merge(x, mod4, norm_g3, wg, oa, ob, oc, wba, wbb, wbc, wo, l, row_of_tile):
    t = x.shape[0]
    row = lambda n: pl.BlockSpec((TM, n), lambda i: (i, 0))
    res = lambda r, c: pl.BlockSpec((None, r, c), lambda i: (l, 0, 0))
    return pl.pallas_call(
        _merge_kernel,
        out_shape=jax.ShapeDtypeStruct((t, D), F32),
        grid=(t // TM,),
        in_specs=[row(D), _mod_spec(l, 0, row_of_tile), _mod_spec(l, 1, row_of_tile),
                  _mod_spec(l, 2, row_of_tile), res(1, D), res(D, 3 * D),
                  row(A_HEADS * LANES), row(B_HEADS * LANES), row(C_HEADS * LANES),
                  res(A_HEADS * LANES, D), res(B_HEADS * LANES, D), res(C_HEADS * LANES, D), res(D, D)],
        out_specs=row(D),
        compiler_params=_cparams(("parallel",)),
        name="merge",
    )(x, mod4, mod4, mod4, norm_g3, wg, oa, ob, oc, wba, wbb, wbc, wo)


def _swiglu(h, wg, wu):
    a = _dot(h, wg)
    return (a * _sigmoid(a) * _dot(h, wu)).astype(BF16)


def _ffn_kernel(tf, x_ref, shift_ref, scale_ref, gate_ref, g_ref, wg_ref, wu_ref, wd_ref, o_ref):
    x = x_ref[...]
    h = _norm_mod(x, g_ref, shift_ref, scale_ref).astype(BF16)
    acc = None
    for c in range(wg_ref.shape[1] // tf):
        sl = slice(c * tf, (c + 1) * tf)
        part = _dot(_swiglu(h, wg_ref[:, sl], wu_ref[:, sl]), wd_ref[sl, :])
        acc = part if acc is None else acc + part
    o_ref[...] = x + gate_ref[...] * acc


def _ffn(x, mod4, norm_g3, wg, wu, wd, l, idx, row_of_tile):
    t = x.shape[0]
    dff = wg.shape[2]
    tf = 256
    assert dff % tf == 0
    row = pl.BlockSpec((TM, D), lambda i: (i, 0))
    return pl.pallas_call(
        functools.partial(_ffn_kernel, tf),
        out_shape=jax.ShapeDtypeStruct((t, D), F32),
        grid=(t // TM,),
        in_specs=[row, _mod_spec(l, 3, row_of_tile), _mod_spec(l, 4, row_of_tile),
                  _mod_spec(l, 5, row_of_tile),
                  pl.BlockSpec((None, 1, D), lambda i: (l, 0, 0)),
                  pl.BlockSpec((None, D, dff), lambda i: (idx, 0, 0)),
                  pl.BlockSpec((None, D, dff), lambda i: (idx, 0, 0)),
                  pl.BlockSpec((None, dff, D), lambda i: (idx, 0, 0))],
        out_specs=row,
        compiler_params=_cparams(("parallel",)),
        name="ffn",
    )(x, mod4, mod4, mod4, norm_g3, wg, wu, wd)


def _top2(lg, axis, n):
    idx = lax.broadcasted_iota(jnp.int32, lg.shape, axis)
    m1 = jnp.max(lg, axis=axis, keepdims=True)
    i1 = jnp.min(jnp.where(lg == m1, idx, n), axis=axis, keepdims=True)
    rest = jnp.where(idx == i1, NEG_INF, lg)
    m2 = jnp.max(rest, axis=axis, keepdims=True)
    i2 = jnp.min(jnp.where(rest == m2, idx, n), axis=axis, keepdims=True)
    return m1, i1, m2, i2


def _pack_halves(hb):
    lo = pltpu.bitcast(hb[:, :D // 2].astype(F32), jnp.uint32)
    hi = pltpu.bitcast(hb[:, D // 2:].astype(F32), jnp.uint32)
    return (lo >> 16) | hi


def _unpack_halves(u):
    lo = pltpu.bitcast(u << 16, F32).astype(BF16)
    hi = pltpu.bitcast(u & jnp.uint32(0xFFFF0000), F32).astype(BF16)
    return lo, hi


def _route_kernel(x_ref, shift_ref, scale_ref, g_ref, wr_ref, hp_ref, wts_ref, meta_ref, cnt_ref, run_s):
    @pl.when(pl.program_id(0) == 0)
    def _():
        run_s[...] = jnp.zeros_like(run_s)

    h = _norm_mod(x_ref[...], g_ref, shift_ref, scale_ref)
    hp_ref[...] = _pack_halves(h.astype(BF16))
    logits = jnp.dot(h, wr_ref[...], preferred_element_type=F32, precision=lax.Precision.HIGHEST)
    lane = lax.broadcasted_iota(jnp.int32, logits.shape, 1)
    m1, _, m2, _ = _top2(jnp.where(lane < N_EXPERTS, logits, NEG_INF), 1, LANES)
    e2 = jnp.exp(m2 - m1)
    den = 1.0 + e2
    wts_ref[...] = jnp.where(lane == 0, 1.0 / den, jnp.where(lane == 1, e2 / den, 0.0))

    lt = logits.T[:N_EXPERTS, :]
    _, i1, _, i2 = _top2(lt, 0, N_EXPERTS)
    sub = lax.broadcasted_iota(jnp.int32, lt.shape, 0)
    onehot = jnp.where((sub == i1) | (sub == i2), 1.0, 0.0)
    src = lax.broadcasted_iota(jnp.int32, (TM, TM), 0)
    dst = lax.broadcasted_iota(jnp.int32, (TM, TM), 1)
    earlier = jnp.where(src < dst, 1.0, 0.0).astype(BF16)
    before = _dot(onehot.astype(BF16), earlier) + run_s[:, 0:1]
    rank1 = jnp.sum(jnp.where(sub == i1, before, 0.0), axis=0, keepdims=True).astype(jnp.int32)
    rank2 = jnp.sum(jnp.where(sub == i2, before, 0.0), axis=0, keepdims=True).astype(jnp.int32)
    meta_ref[...] = jnp.where(sub == 0, i1, jnp.where(sub == 1, i2, jnp.where(sub == 2, rank1,
                              jnp.where(sub == 3, rank2, 0))))
    run_s[...] = run_s[...] + jnp.sum(onehot, axis=1, keepdims=True)
    cnt_ref[...] = run_s[...]


def _route(x, mod4, norm_g3, wr, l, idx, row_of_tile):
    t = x.shape[0]
    nt = t // TM
    return pl.pallas_call(
        _route_kernel,
        out_shape=[jax.ShapeDtypeStruct((t, D // 2), jnp.uint32), jax.ShapeDtypeStruct((t, LANES), F32),
                   jax.ShapeDtypeStruct((nt, N_EXPERTS, TM), jnp.int32),
                   jax.ShapeDtypeStruct((N_EXPERTS, LANES), F32)],
        grid=(nt,),
        in_specs=[pl.BlockSpec((TM, D), lambda i: (i, 0)),
                  _mod_spec(l, 3, row_of_tile), _mod_spec(l, 4, row_of_tile),
                  pl.BlockSpec((None, 1, D), lambda i: (l, 0, 0)),
                  pl.BlockSpec((None, D, LANES), lambda i: (idx, 0, 0))],
        out_specs=[pl.BlockSpec((TM, D // 2), lambda i: (i, 0)), pl.BlockSpec((TM, LANES), lambda i: (i, 0)),
                   pl.BlockSpec((None, N_EXPERTS, TM), lambda i: (i, 0, 0)),
                   pl.BlockSpec((N_EXPERTS, LANES), lambda i: (0, 0))],
        scratch_shapes=[pltpu.VMEM((N_EXPERTS, LANES), F32)],
        compiler_params=_cparams(("arbitrary",)),
        name="moe_route",
    )(x, mod4, mod4, norm_g3, wr)


def _row_copy(src, src_row, dst, dst_row, sem):
    return pltpu.make_async_copy(src.at[pl.ds(src_row, 1)], dst.at[pl.ds(dst_row, 1)], sem)


ROWS_PER_ISSUE = 8


def _for_each_assignment(fn):
    def trip(g, carry):
        row0 = pl.multiple_of(g * ROWS_PER_ISSUE, ROWS_PER_ISSUE)
        for j in range(ROWS_PER_ISSUE):
            for k in range(2):
                fn(row0 + j, k, 2 * row0 + (2 * j + k))
        return carry

    lax.fori_loop(0, TM // ROWS_PER_ISSUE, trip, 0)


def _scatter_kernel(dest_ref, zero_blk_ref, hp_ref, xs_ref, zero_s, sem):
    @pl.when(pl.program_id(0) == 0)
    def _():
        zero_s[...] = jnp.zeros_like(zero_s)

        def block_copy(b):
            return pltpu.make_async_copy(zero_s, xs_ref.at[pl.ds(pl.multiple_of(b * BM, BM), BM)], sem)

        for j in range(zero_blk_ref.shape[1]):
            @pl.when(zero_blk_ref[1, j] == 1)
            def _(j=j):
                block_copy(zero_blk_ref[0, j]).start()
        for j in range(zero_blk_ref.shape[1]):
            @pl.when(zero_blk_ref[1, j] == 1)
            def _():
                block_copy(0).wait()

    _for_each_assignment(lambda r, k, f: _row_copy(hp_ref, r, xs_ref, dest_ref[0, f], sem).start())
    _for_each_assignment(lambda r, k, f: _row_copy(hp_ref, 0, xs_ref, 0, sem).wait())


def _scatter(dest, tails, hp, rows):
    nt = dest.shape[0]
    return pl.pallas_call(
        _scatter_kernel,
        out_shape=jax.ShapeDtypeStruct((rows, D // 2), jnp.uint32),
        grid=(nt,),
        in_specs=[pl.BlockSpec((None, 1, 2 * TM), lambda i: (i, 0, 0), memory_space=pltpu.SMEM),
                  pl.BlockSpec(memory_space=pltpu.SMEM),
                  pl.BlockSpec((TM, D // 2), lambda i: (i, 0))],
        out_specs=pl.BlockSpec(memory_space=pl.ANY),
        scratch_shapes=[pltpu.VMEM((BM, D // 2), jnp.uint32), pltpu.SemaphoreType.DMA(())],
        compiler_params=_cparams(("arbitrary",)),
        name="moe_scatter",
    )(dest, tails, hp)


def _experts_kernel(blk_in_ref, blk_e_ref, n_used_ref, xs_ref, wg_ref, wu_ref, wd_ref, ys_ref):
    del blk_in_ref, blk_e_ref
    used = pl.program_id(0) < n_used_ref[0]

    @pl.when(used)
    def _():
        lo, hi = _unpack_halves(xs_ref[...])
        half = D // 2
        a = _dot(lo, wg_ref[:half, :]) + _dot(hi, wg_ref[half:, :])
        b = _dot(lo, wu_ref[:half, :]) + _dot(hi, wu_ref[half:, :])
        ys_ref[...] = _dot((a * _sigmoid(a) * b).astype(BF16), wd_ref[...])

    @pl.when(jnp.logical_not(used))
    def _():
        ys_ref[...] = jnp.zeros_like(ys_ref)


def _experts(xs, tables, wg, wu, wd, idx):
    nblk = xs.shape[0] // BM
    dfe = wg.shape[3]
    grid_spec = pltpu.PrefetchScalarGridSpec(
        num_scalar_prefetch=3,
        grid=(nblk,),
        in_specs=[pl.BlockSpec((BM, D // 2), lambda i, bi, be, nu: (bi[i], 0)),
                  pl.BlockSpec((None, None, D, dfe), lambda i, bi, be, nu: (idx, be[i], 0, 0)),
                  pl.BlockSpec((None, None, D, dfe), lambda i, bi, be, nu: (idx, be[i], 0, 0)),
                  pl.BlockSpec((None, None, dfe, D), lambda i, bi, be, nu: (idx, be[i], 0, 0))],
        out_specs=pl.BlockSpec((BM, D), lambda i, bi, be, nu: (i, 0)),
    )
    return pl.pallas_call(
        _experts_kernel,
        out_shape=jax.ShapeDtypeStruct((nblk * BM, D), F32),
        grid_spec=grid_spec,
        compiler_params=_cparams(("arbitrary",)),
        name="moe_experts",
    )(*tables, xs, wg, wu, wd)


def _combine_kernel(dest_ref, x_ref, gate_ref, wts_ref, ys_ref, o_ref, buf, sem):
    _for_each_assignment(lambda r, k, f: _row_copy(ys_ref, dest_ref[0, f], buf.at[k], r, sem).start())
    _for_each_assignment(lambda r, k, f: _row_copy(ys_ref, 0, buf.at[k], 0, sem).wait())
    w = wts_ref[...]
    y = w[:, 0:1] * buf[0] + w[:, 1:2] * buf[1]
    o_ref[...] = x_ref[...] + gate_ref[...] * y


def _combine(dest, x, mod4, wts, ys, l, row_of_tile):
    t = x.shape[0]
    row = pl.BlockSpec((TM, D), lambda i: (i, 0))
    return pl.pallas_call(
        _combine_kernel,
        out_shape=jax.ShapeDtypeStruct((t, D), F32),
        grid=(t // TM,),
        in_specs=[pl.BlockSpec((None, 1, 2 * TM), lambda i: (i, 0, 0), memory_space=pltpu.SMEM),
                  row, _mod_spec(l, 5, row_of_tile),
                  pl.BlockSpec((TM, LANES), lambda i: (i, 0)),
                  pl.BlockSpec(memory_space=pl.ANY)],
        out_specs=row,
        scratch_shapes=[pltpu.VMEM((2, TM, D), F32), pltpu.SemaphoreType.DMA(())],
        compiler_params=_cparams(("arbitrary",)),
        name="moe_combine",
    )(dest, x, mod4, wts, ys)


def _moe(x, mod4, norm_g3, wr, wg, wu, wd, l, idx, row_of_tile):
    t = x.shape[0]
    nblk = 2 * t // BM + N_EXPERTS
    hp, wts, meta, cnt = _route(x, mod4, norm_g3, wr, l, idx, row_of_tile)
    counts = cnt[:, 0].astype(jnp.int32)
    blocks = (counts + BM - 1) // BM
    ends = jnp.cumsum(blocks)
    n_used = ends[-1]
    base = (ends - blocks) * BM
    experts = jnp.arange(N_EXPERTS, dtype=jnp.int32)

    def base_of(e):
        return jnp.sum(jnp.where(e[..., None] == experts, base, 0), axis=-1)

    dest = jnp.stack([base_of(meta[:, 0]) + meta[:, 2], base_of(meta[:, 1]) + meta[:, 3]], axis=-1)
    dest = dest.reshape(dest.shape[0], 1, 2 * TM)
    step = jnp.arange(nblk, dtype=jnp.int32)
    blk_in = jnp.minimum(step, jnp.maximum(n_used - 1, 0))
    blk_e = jnp.minimum(jnp.sum(blk_in[:, None] >= ends[None, :], axis=1), N_EXPERTS - 1).astype(jnp.int32)
    tables = (blk_in, blk_e, n_used.reshape(1))
    zero_blk = jnp.stack([
        jnp.concatenate([jnp.maximum(ends - 1, 0), jnp.minimum(n_used + experts, nblk - 1)]),
        jnp.concatenate([blocks > 0, n_used + experts < nblk]).astype(jnp.int32)]).astype(jnp.int32)
    xs = _scatter(dest, zero_blk, hp, nblk * BM)
    ys = _experts(xs, tables, wg, wu, wd, idx)
    return _combine(dest, x, mod4, wts, ys, l, row_of_tile)


def _final_kernel(x_ref, g_ref, o_ref):
    o_ref[...] = _rms(x_ref[...], g_ref[...])


def _final_norm(x, g2):
    t = x.shape[0]
    return pl.pallas_call(
        _final_kernel,
        out_shape=jax.ShapeDtypeStruct((t, D), F32),
        grid=(t // TM,),
        in_specs=[pl.BlockSpec((TM, D), lambda i: (i, 0)), pl.BlockSpec((1, D), lambda i: (0, 0))],
        out_specs=pl.BlockSpec((TM, D), lambda i: (i, 0)),
        compiler_params=_cparams(("parallel",)),
        name="final_norm",
    )(x, g2)


def _relayout_w_in(w_in):
    depth = w_in.shape[0]
    z = lambda n: jnp.zeros((depth, D, n), w_in.dtype)
    aq, ak, av = w_in[..., 0:512], w_in[..., 512:1024], w_in[..., 1024:1536]
    bq, bk, bv = w_in[..., 1536:2048], w_in[..., 2048:2176], w_in[..., 2176:2304]
    cq, ckv, ckr = w_in[..., 2304:2560], w_in[..., 2560:2688], w_in[..., 2688:2720]
    gates = w_in[..., 2720:2720 + 3 * D]
    pieces = [aq, ak, av]
    for h in range(B_HEADS):
        qh = bq[..., h * HD:(h + 1) * HD]
        pieces += [qh, z(HD)] if h // B_GROUP == 0 else [z(HD), qh]
    pieces += [bk, bv, cq, ckv, z(KR_OFF), ckr, z(LANES - KR_OFF - C_ROPE)]
    w_a = jnp.concatenate(pieces, axis=-1).astype(BF16)
    assert w_a.shape[-1] == NPA
    return w_a, gates.astype(BF16)


def _relayout_mla(w_q_up, w_kv_up):
    depth = w_q_up.shape[0]
    qd = C_NOPE + C_ROPE
    wq = jnp.pad(w_q_up.reshape(depth, C_Q_LORA, C_HEADS, qd), ((0, 0), (0, 0), (0, 0), (0, LANES - qd)))
    kv = w_kv_up.reshape(depth, C_KV_LORA, C_HEADS, C_NOPE + C_VDIM)
    wkn = jnp.pad(kv[..., :C_NOPE], ((0, 0), (0, 0), (0, 0), (0, LANES - C_NOPE)))
    wv = jnp.pad(kv[..., C_NOPE:], ((0, 0), (0, 0), (0, 0), (0, LANES - C_VDIM)))
    flat = lambda w: w.reshape(depth, w.shape[1], C_HEADS * LANES).astype(BF16)
    return flat(wq), flat(wkn), flat(wv)


def _relayout_w_branch(w_branch):
    depth = w_branch.shape[0]
    wba = w_branch[:, 0]
    b = w_branch[:, 1].reshape(depth, B_HEADS, HD, D)
    zb = jnp.zeros_like(b)
    lo = jnp.concatenate([b, zb], axis=2)
    hi = jnp.concatenate([zb, b], axis=2)
    grp = (jnp.arange(B_HEADS) // B_GROUP).reshape(1, B_HEADS, 1, 1)
    wbb = jnp.where(grp == 0, lo, hi).reshape(depth, B_HEADS * LANES, D)
    c = w_branch[:, 2].reshape(depth, C_HEADS, C_VDIM, D)
    wbc = jnp.pad(c, ((0, 0), (0, 0), (0, LANES - C_VDIM), (0, 0))).reshape(depth, C_HEADS * LANES, D)
    return wba.astype(BF16), wbb.astype(BF16), wbc.astype(BF16)


def _rope_tables(n_tokens, rot_dim, lane_off, group):
    rows_n = n_tokens // GRID_W
    row, col = jnp.meshgrid(jnp.arange(rows_n), jnp.arange(GRID_W), indexing="ij")
    row = row.reshape(-1).astype(F32)
    col = col.reshape(-1).astype(F32)
    n_freq = rot_dim // 4
    inv = jnp.power(ROPE_THETA, -jnp.arange(n_freq, dtype=F32) / n_freq)
    ang = jnp.concatenate([row[:, None] * inv, col[:, None] * inv], axis=-1)
    cos, sin = jnp.cos(ang), jnp.sin(ang)
    half = rot_dim // 2
    c = jnp.ones((n_tokens, LANES), F32)
    sa = jnp.zeros((n_tokens, LANES), F32)
    sb = jnp.zeros((n_tokens, LANES), F32)
    for off in range(lane_off, LANES - rot_dim + 1, group):
        c = c.at[:, off:off + half].set(cos).at[:, off + half:off + rot_dim].set(cos)
        sa = sa.at[:, off:off + half].set(-sin)
        sb = sb.at[:, off + half:off + rot_dim].set(sin)
        if group >= LANES:
            break
    ident = (jnp.ones((TM, LANES), F32), jnp.zeros((TM, LANES), F32), jnp.zeros((TM, LANES), F32))
    return tuple(jnp.concatenate([a, i], axis=0) for a, i in zip((c, sa, sb), ident))


def _run_group(x, nb, s, row_of_tile, tab_block, tabs, caches, mod4, w):
    depth = w["w_a"].shape[0]
    t = nb * s
    assert t % TM == 0 and (s % TM == 0 or TM % s == 0)
    pas, ckvns = [], []
    for l in range(depth):
        qa, ka, va, qb, kb, vb, qc, kc, vc, ckvn, *raw = _proj(
            x, mod4, w["norm_attn_g"], w["w_a"], tabs, w["gq"], w["gkv"], w["wq"], w["wkn"], w["wv"],
            l, row_of_tile, tab_block, caches is None)
        if caches is None:
            ctx_a = ctx_b = ctx_c = None
            dk, dv, kv_b, kr = raw
            pas.append((dk, dv, kv_b[:, :LANES], kv_b[:, LANES:], kr[:, KR_OFF:KR_OFF + C_ROPE]))
            ckvns.append(ckvn)
        else:
            ctx_a, ctx_b, ctx_c = caches
        oa = _attn_a(qa, ka, va, ctx_a, w["lam4"], w["subg"], l, nb, s)
        ob = _attn_b(qb, kb, vb, ctx_b, w["win_sink"], l, nb, s)
        oc = _attn_c(qc, kc, vc, ctx_c, w["wkn"], w["wv"], l, nb, s)
        x = _merge(x, mod4, w["norm_attn_g"], w["w_gates"], oa, ob, oc,
                   w["wba"], w["wbb"], w["wbc"], w["w_out"], l, row_of_tile)
        if l % 2 == 0:
            x = _ffn(x, mod4, w["norm_ffn_g"], w["ffn_g"], w["ffn_u"], w["ffn_d"], l, l // 2, row_of_tile)
        else:
            x = _moe(x, mod4, w["norm_ffn_g"], w["moe_r"], w["moe_g"], w["moe_u"], w["moe_d"],
                     l, l // 2, row_of_tile)
    return _final_norm(x, w["final_g"]), pas, ckvns


def kernel(x_prompt, x_sample, cache_diff_k, cache_diff_v, cache_win_k, cache_win_v, cache_mla_ckv, cache_mla_krope, c, c_ctx, w_ada, b_ada, norm_attn_g, norm_ffn_g, w_in, diff_lambda_q1, diff_lambda_k1, diff_lambda_q2, diff_lambda_k2, diff_subln_g, win_sink, mla_q_norm_g, mla_w_q_up, mla_kv_norm_g, mla_w_kv_up, w_branch, w_out, ffn_w_gate, ffn_w_up, ffn_w_down, moe_w_router, moe_w_gate, moe_w_up, moe_w_down, final_norm_g):
    depth = w_in.shape[0]
    nbc, sc, _ = x_prompt.shape
    nbl, sl, _ = x_sample.shape
    past = cache_diff_k.shape[2]
    assert 1 + nbl <= MOD_ROWS

    cond = jnp.zeros((MOD_ROWS, D), F32).at[0].set(c_ctx).at[1:1 + nbl].set(c)
    mod4 = _ada_mod(cond, w_ada, b_ada).reshape(depth, MOD_ROWS, 1, 6 * D)

    w_a, w_gates = _relayout_w_in(w_in)
    wq, wkn, wv = _relayout_mla(mla_w_q_up, mla_w_kv_up)
    wba, wbb, wbc = _relayout_w_branch(w_branch)
    vec3 = lambda a: a.reshape(a.shape[0], 1, a.shape[1])
    w = dict(
        w_a=w_a, w_gates=w_gates, wq=wq, wkn=wkn, wv=wv, wba=wba, wbb=wbb, wbc=wbc,
        w_out=w_out.astype(BF16), norm_attn_g=vec3(norm_attn_g), norm_ffn_g=vec3(norm_ffn_g),
        gq=vec3(mla_q_norm_g), gkv=vec3(mla_kv_norm_g), subg=vec3(diff_subln_g),
        lam4=[vec3(a) for a in (diff_lambda_q1, diff_lambda_k1, diff_lambda_q2, diff_lambda_k2)],
        win_sink=win_sink,
        ffn_g=ffn_w_gate.astype(BF16), ffn_u=ffn_w_up.astype(BF16), ffn_d=ffn_w_down.astype(BF16),
        moe_r=jnp.pad(moe_w_router, ((0, 0), (0, 0), (0, LANES - N_EXPERTS))),
        moe_g=moe_w_gate.astype(BF16), moe_u=moe_w_up.astype(BF16), moe_d=moe_w_down.astype(BF16),
        final_g=final_norm_g.reshape(1, D),
    )

    tabs_h = _rope_tables(sl, HD, 0, HD)
    tabs_r = _rope_tables(sl, C_ROPE, KR_OFF, LANES)
    tabs = tabs_h + tabs_r
    ident_block = sl // TM

    y_c, pas, ckvns = _run_group(
        x_prompt.reshape(nbc * sc, D), nbc, sc, lambda i: 0, lambda i: ident_block, tabs, None, mod4, w)

    tiles_per_b = sl // TM
    krp = jnp.pad(cache_mla_krope, ((0, 0), (0, 0), (0, 0), (KR_OFF, LANES - KR_OFF - C_ROPE)))
    caches = (
        (cache_diff_k.reshape(nbl, depth, past, A_HEADS * LANES),
         cache_diff_v.reshape(nbl, depth, past, A_HEADS * LANES)),
        (cache_win_k.reshape(nbl, depth, past, LANES), cache_win_v.reshape(nbl, depth, past, LANES)),
        (cache_mla_ckv, krp),
    )
    y_l, _, _ = _run_group(
        x_sample.reshape(nbl * sl, D), nbl, sl, lambda i: 1 + i // tiles_per_b,
        lambda i: i % tiles_per_b, tabs, caches, mod4, w)

    def cache(per_layer, tail):
        return jnp.stack([a.reshape((nbc, sc) + tail) for a in per_layer], axis=1)

    new_diff_k = cache([p[0] for p in pas], (A_HEADS, 2 * HD))
    new_diff_v = cache([p[1] for p in pas], (A_HEADS, 2 * HD))
    new_win_k = cache([p[2] for p in pas], (B_HEADS // B_GROUP, HD))
    new_win_v = cache([p[3] for p in pas], (B_HEADS // B_GROUP, HD))
    new_mla_ckv = cache(ckvns, (C_KV_LORA,))
    new_mla_krope = cache([p[4] for p in pas], (C_ROPE,))
    return (y_c.reshape(nbc, sc, D), y_l.reshape(nbl, sl, D), new_diff_k, new_diff_v,
            new_win_k, new_win_v, new_mla_ckv, new_mla_krope)
```
